```python
import jax, jax.numpy as jnp
from jax import lax
import numpy as np

D_MODEL = 2048
BATCH = 4
SEQ = 2048
DEPTH = 1
DEC_BATCH = 32
DEC_SEQ = 16
PAST_LEN = 1024

CHUNK = 64
MIX_WIDTH = D_MODEL
CONV_WIDTH = MIX_WIDTH // 2
CONV_KERNEL = 31
CONV_STATE = CONV_KERNEL - 1
HG_WIDTH = MIX_WIDTH - CONV_WIDTH
HG_HEAD_DIM = 128
HG_HEADS = HG_WIDTH // HG_HEAD_DIM
HG_BLOCK = CHUNK // 4
D_FF = ((8 * D_MODEL + 767) // 768) * 256
IN_COLS = 2 * CONV_WIDTH + 4 * HG_WIDTH
SPLITS = [CONV_WIDTH, 2 * CONV_WIDTH, 2 * CONV_WIDTH + HG_WIDTH,
          2 * CONV_WIDTH + 2 * HG_WIDTH, 2 * CONV_WIDTH + 3 * HG_WIDTH]
ALPHA = (2.0 * DEPTH) ** 0.25
BETA = (8.0 * DEPTH) ** -0.25
LN_EPS = 1e-5

kernel_name = "hymba_conformer_hgrn2_stream_step"


def layer_norm(x, g, b):
    xf = x.astype(jnp.float32)
    mu = jnp.mean(xf, -1, keepdims=True)
    var = jnp.mean(jnp.square(xf - mu), -1, keepdims=True)
    return ((xf - mu) * lax.rsqrt(var + LN_EPS) * g.astype(jnp.float32)
            + b.astype(jnp.float32)).astype(x.dtype)


def rms_norm(x, g):
    xf = x.astype(jnp.float32)
    return xf * lax.rsqrt(jnp.mean(jnp.square(xf), -1, keepdims=True) + LN_EPS) * g.astype(jnp.float32)


def conv_mixer(a, gate, hist, w_dw, b_dw, n_g, n_b):
    u = a * jax.nn.sigmoid(gate)
    u_ext = jnp.concatenate([hist.astype(u.dtype), u], axis=1)
    h = lax.conv_general_dilated(
        u_ext, w_dw[:, None, :].astype(u.dtype), window_strides=(1,), padding='VALID',
        dimension_numbers=('NWC', 'WIO', 'NWC'), feature_group_count=u.shape[-1])
    h = h + b_dw
    h = jax.nn.silu(layer_norm(h, n_g, n_b))
    return h, u_ext[:, -CONV_STATE:]


def hgrn2_block(S0, q, k, v, logf):
    L = q.shape[2]
    lc = jnp.cumsum(logf, axis=2)
    causal = jnp.tril(jnp.ones((L, L), dtype=bool))
    diff = lc[:, :, :, None, :] - lc[:, :, None, :, :]
    decay = jnp.exp(jnp.where(causal[None, None, :, :, None], diff, -jnp.inf))
    scores = jnp.einsum('bhtk,bhsk,bhtsk->bhts', q, k, decay)
    o = (jnp.einsum('bhts,bhsv->bhtv', scores, v)
         + jnp.einsum('bhtk,bhkv->bhtv', q * jnp.exp(lc), S0))
    lc_last = lc[:, :, -1:, :]
    k_to_end = k * jnp.exp(lc_last - lc)
    S1 = jnp.exp(lc_last[:, :, 0, :])[..., None] * S0 + jnp.einsum('bhsk,bhsv->bhkv', k_to_end, v)
    return S1, o


def hgrn2_scan(S0, q, k, v, logf):
    B, H, T, _ = q.shape
    n = T // HG_BLOCK

    def to_blocks(z):
        return jnp.moveaxis(z.reshape(B, H, n, HG_BLOCK, z.shape[-1]), 2, 0)

    def step(S, blk):
        return hgrn2_block(S, blk[0], blk[1], blk[2], blk[3])

    S_fin, o = lax.scan(step, S0, (to_blocks(q), to_blocks(k), to_blocks(v), to_blocks(logf)))
    o = jnp.moveaxis(o, 0, 2).reshape(B, H, T, HG_HEAD_DIM)
    return S_fin, o


def hgrn2_mixer(q_raw, f_raw, i_raw, g_raw, S0, lb, norm_g, seq_fn):
    B, T, _ = q_raw.shape

    def heads(z):
        return z.astype(jnp.float32).reshape(B, T, HG_HEADS, HG_HEAD_DIM).transpose(0, 2, 1, 3)

    f = lb + (1.0 - lb) * jax.nn.sigmoid(f_raw.astype(jnp.float32))
    f = heads(f)
    q = jax.nn.silu(heads(q_raw))
    S1, o = seq_fn(S0.astype(jnp.float32), q, 1.0 - f, heads(i_raw), jnp.log(f))
    o = rms_norm(o, norm_g)
    o = o.transpose(0, 2, 1, 3).reshape(B, T, HG_WIDTH) * jax.nn.silu(g_raw.astype(jnp.float32))
    return o.astype(q_raw.dtype), S1


def encoder_layer(x, conv_hist, hg_state, seq_fn, w_in, b_in, w_dw, b_dw, cn_g, cn_b, lb,
                  hg_norm_g, w_out, ln1_g, ln1_b, w_gate, w_up, w_down, ln2_g, ln2_b):
    proj = x @ w_in + b_in
    a, gate, q_raw, f_raw, i_raw, g_raw = jnp.split(proj, SPLITS, axis=-1)
    conv_out, conv_new = conv_mixer(a, gate, conv_hist, w_dw, b_dw, cn_g, cn_b)
    hg_out, hg_new = hgrn2_mixer(q_raw, f_raw, i_raw, g_raw, hg_state, lb, hg_norm_g, seq_fn)
    mixed = jnp.concatenate([conv_out, hg_out], axis=-1) @ w_out
    h = layer_norm(ALPHA * x + mixed, ln1_g, ln1_b)
    ffn = (jax.nn.silu(h @ w_gate) * (h @ w_up)) @ w_down
    y = layer_norm(ALPHA * h + ffn, ln2_g, ln2_b)
    return y, conv_new, hg_new


def setup_inputs(seed: int = 0) -> dict:
    key = jax.random.key(seed)
    ks = jax.random.split(key, 24)
    nrm = lambda k, s, sc: jax.random.normal(k, s, jnp.float32) * sc
    return {
        "x_prompt": nrm(ks[0], (BATCH, SEQ, D_MODEL), 1.0),
        "x_sample": nrm(ks[1], (DEC_BATCH, DEC_SEQ, D_MODEL), 1.0),
        "cache_conv": nrm(ks[2], (DEPTH, DEC_BATCH, CONV_STATE, CONV_WIDTH), 0.5),
        "state_hgrn": nrm(ks[3], (DEPTH, DEC_BATCH, HG_HEADS, HG_HEAD_DIM, HG_HEAD_DIM), 0.5),
        "w_in": nrm(ks[4], (DEPTH, D_MODEL, IN_COLS), D_MODEL ** -0.5),
        "b_in": nrm(ks[5], (DEPTH, IN_COLS), 0.02),
        "w_dw": nrm(ks[6], (DEPTH, CONV_KERNEL, CONV_WIDTH), CONV_KERNEL ** -0.5),
        "b_dw": nrm(ks[7], (DEPTH, CONV_WIDTH), 0.02),
        "conv_norm_g": 1.0 + nrm(ks[8], (DEPTH, CONV_WIDTH), 0.02),
        "conv_norm_b": nrm(ks[9], (DEPTH, CONV_WIDTH), 0.02),
        "hg_lower_bounds": nrm(ks[10], (DEPTH + 1, HG_WIDTH), 0.5),
        "hg_norm_g": 1.0 + nrm(ks[11], (DEPTH, HG_HEAD_DIM), 0.02),
        "w_out": nrm(ks[12], (DEPTH, MIX_WIDTH, D_MODEL), BETA * MIX_WIDTH ** -0.5),
        "ln1_g": 1.0 + nrm(ks[13], (DEPTH, D_MODEL), 0.02),
        "ln1_b": nrm(ks[14], (DEPTH, D_MODEL), 0.02),
        "w_gate": nrm(ks[15], (DEPTH, D_MODEL, D_FF), D_MODEL ** -0.5),
        "w_up": nrm(ks[16], (DEPTH, D_MODEL, D_FF), D_MODEL ** -0.5),
        "w_down": nrm(ks[17], (DEPTH, D_FF, D_MODEL), BETA * D_FF ** -0.5),
        "ln2_g": 1.0 + nrm(ks[18], (DEPTH, D_MODEL), 0.02),
        "ln2_b": nrm(ks[19], (DEPTH, D_MODEL), 0.02),
    }


def reference(x_prompt, x_sample, cache_conv, state_hgrn, w_in, b_in, w_dw, b_dw, conv_norm_g,
              conv_norm_b, hg_lower_bounds, hg_norm_g, w_out, ln1_g, ln1_b, w_gate, w_up, w_down,
              ln2_g, ln2_b):
    lb_all = jnp.cumsum(jax.nn.softmax(hg_lower_bounds.astype(jnp.float32), axis=0), axis=0)
    yp, ys = x_prompt, x_sample
    conv_p, hg_p, conv_s, hg_s = [], [], [], []
    for l in range(DEPTH):
        params = (w_in[l], b_in[l], w_dw[l], b_dw[l], conv_norm_g[l], conv_norm_b[l], lb_all[l],
                  hg_norm_g[l], w_out[l], ln1_g[l], ln1_b[l], w_gate[l], w_up[l], w_down[l],
                  ln2_g[l], ln2_b[l])
        hist0 = jnp.zeros((yp.shape[0], CONV_STATE, CONV_WIDTH), yp.dtype)
        S0 = jnp.zeros((yp.shape[0], HG_HEADS, HG_HEAD_DIM, HG_HEAD_DIM), jnp.float32)
        yp, cp, sp = encoder_layer(yp, hist0, S0, hgrn2_scan, *params)
        ys, cs, ss = encoder_layer(ys, cache_conv[l], state_hgrn[l], hgrn2_block, *params)
        conv_p.append(cp)
        hg_p.append(sp.astype(x_prompt.dtype))
        conv_s.append(cs.astype(cache_conv.dtype))
        hg_s.append(ss.astype(state_hgrn.dtype))
    return (yp, ys, jnp.stack(conv_p), jnp.stack(hg_p), jnp.stack(conv_s), jnp.stack(hg_s))
```

```python
import functools

import jax
import jax.numpy as jnp
from jax import lax
from jax.experimental import pallas as pl
from jax.experimental.pallas import tpu as pltpu

LN_EPS = 1e-5
HG_HEAD_DIM = 128
HG_BLOCK = 16
CONV_HALO = 32
VMEM_LIMIT = 56 * 1024 * 1024
ROW_TILE = 512

F32 = jnp.float32
BF16 = jnp.bfloat16


def _sigmoid(x):
    return 1.0 / (1.0 + jnp.exp(-x))


def _silu(x):
    return x * _sigmoid(x)


def _layer_norm(x, g, b):
    mu = jnp.mean(x, axis=-1, keepdims=True)
    xc = x - mu
    var = jnp.mean(xc * xc, axis=-1, keepdims=True)
    return xc * lax.rsqrt(var + LN_EPS) * g + b


def _params(sem):
    return pltpu.CompilerParams(dimension_semantics=sem, vmem_limit_bytes=VMEM_LIMIT)


def _in_proj_kernel(xp_ref, xs_ref, w_ref, b_ref, op_ref, os_ref, xb_ref, *, n_prompt_tiles):
    i = pl.program_id(0)
    j = pl.program_id(1)

    @pl.when((j == 0) & (i < n_prompt_tiles))
    def _():
        xb_ref[...] = xp_ref[...].astype(BF16)

    @pl.when((j == 0) & (i >= n_prompt_tiles))
    def _():
        xb_ref[...] = xs_ref[...].astype(BF16)

    def compute():
        return jnp.dot(xb_ref[...], w_ref[...], preferred_element_type=F32) + b_ref[...]

    @pl.when(i < n_prompt_tiles)
    def _():
        op_ref[...] = compute()

    @pl.when(i >= n_prompt_tiles)
    def _():
        os_ref[...] = compute()


def _in_proj(xp, xs, w_bf, b, tn=1024):
    mp, d = xp.shape
    ms = xs.shape[0]
    n = w_bf.shape[1]
    tm = ROW_TILE
    npt, nst = mp // tm, ms // tm
    grid = (npt + nst, n // tn)
    p_idx = lambda i, j: (jnp.minimum(i, npt - 1), 0)
    s_idx = lambda i, j: (jnp.maximum(i - npt, 0), 0)
    return pl.pallas_call(
        functools.partial(_in_proj_kernel, n_prompt_tiles=npt),
        grid=grid,
        in_specs=[
            pl.BlockSpec((tm, d), p_idx),
            pl.BlockSpec((tm, d), s_idx),
            pl.BlockSpec((d, tn), lambda i, j: (0, j)),
            pl.BlockSpec((1, tn), lambda i, j: (0, j)),
        ],
        out_specs=[
            pl.BlockSpec((tm, tn), lambda i, j: (jnp.minimum(i, npt - 1), jnp.where(i < npt, j, n // tn - 1))),
            pl.BlockSpec((tm, tn), lambda i, j: (jnp.maximum(i - npt, 0), jnp.where(i < npt, 0, j))),
        ],
        out_shape=[jax.ShapeDtypeStruct((mp, n), F32), jax.ShapeDtypeStruct((ms, n), F32)],
        scratch_shapes=[pltpu.VMEM((tm, d), BF16)],
        compiler_params=_params(("arbitrary", "arbitrary")),
        name="in_proj",
    )(xp, xs, w_bf, b)


def _conv_kernel(a_ref, gt_ref, hist_ref, w_ref, b_ref, ng_ref, nb_ref, out_ref, tail_ref, buf_ref,
                 *, tt, nt, taps, rows_sub):
    t = pl.program_id(1)

    @pl.when(t == 0)
    def _():
        buf_ref[0:CONV_HALO, :] = hist_ref[0]

    buf_ref[CONV_HALO:CONV_HALO + tt, :] = a_ref[...] * _sigmoid(gt_ref[...])

    lead = CONV_HALO - (taps - 1)
    for r in range(tt // rows_sub):
        base = r * rows_sub
        acc = jnp.broadcast_to(b_ref[...], (rows_sub, b_ref.shape[1]))
        for j in range(taps):
            lo = base + lead + j
            acc = acc + w_ref[j:j + 1, :] * buf_ref[lo:lo + rows_sub, :]
        hn = _layer_norm(acc, ng_ref[...], nb_ref[...])
        out_ref[base:base + rows_sub, :] = _silu(hn).astype(out_ref.dtype)

    tail = buf_ref[tt:tt + CONV_HALO, :]

    @pl.when(t == nt - 1)
    def _():
        tail_ref[0] = tail

    if nt > 1:
        buf_ref[0:CONV_HALO, :] = tail


def _conv_branch(proj, hist_pad, w_dw, b_dw, ng, nb, *, nb_batch, seq, tt):
    taps, c = w_dw.shape
    nt = seq // tt
    rows_sub = min(tt, 16)
    row_idx = lambda b, t: (b * nt + t, 0)
    return pl.pallas_call(
        functools.partial(_conv_kernel, tt=tt, nt=nt, taps=taps, rows_sub=rows_sub),
        grid=(nb_batch, nt),
        in_specs=[
            pl.BlockSpec((tt, c), row_idx),
            pl.BlockSpec((tt, c), lambda b, t: (b * nt + t, 1)),
            pl.BlockSpec((1, CONV_HALO, c), lambda b, t: (b, 0, 0)),
            pl.BlockSpec((taps, c), lambda b, t: (0, 0)),
            pl.BlockSpec((1, c), lambda b, t: (0, 0)),
            pl.BlockSpec((1, c), lambda b, t: (0, 0)),
            pl.BlockSpec((1, c), lambda b, t: (0, 0)),
        ],
        out_specs=[
            pl.BlockSpec((tt, c), row_idx),
            pl.BlockSpec((1, CONV_HALO, c), lambda b, t: (b, 0, 0)),
        ],
        out_shape=[
            jax.ShapeDtypeStruct((nb_batch * seq, c), BF16),
            jax.ShapeDtypeStruct((nb_batch, CONV_HALO, c), F32),
        ],
        scratch_shapes=[pltpu.VMEM((CONV_HALO + tt, c), F32)],
        compiler_params=_params(("arbitrary", "arbitrary")),
        name="conv_branch",
    )(proj, proj, hist_pad, w_dw, b_dw, ng, nb)


def _split3(x):
    hi = x.astype(BF16)
    r1 = x - hi.astype(F32)
    mid = r1.astype(BF16)
    lo = (r1 - mid.astype(F32)).astype(BF16)
    return hi, mid, lo


def _hgrn_kernel(q_ref, f_ref, v_ref, g_ref, s0_ref, lbp_ref, ng_ref, out_ref, sout_ref,
                 st_ref, lc_ref, q_s, k_s, o_s, *, tt, nt, heads, chunk):
    t = pl.program_id(1)
    hd = HG_HEAD_DIM

    @pl.when(t == 0)
    def _():
        for h in range(heads):
            st_ref[h] = s0_ref[0, h].T

    lbp = lbp_ref[...]
    e = jnp.exp(lbp - jnp.max(lbp, axis=0, keepdims=True))
    lb = e[0:1, :] / jnp.sum(e, axis=0, keepdims=True)

    f = lb + (1.0 - lb) * _sigmoid(f_ref[...])
    k_s[...] = 1.0 - f
    q_s[...] = _silu(q_ref[...])
    logf = jnp.log(f)

    ri = lax.broadcasted_iota(jnp.int32, (chunk, chunk), 0)
    ci = lax.broadcasted_iota(jnp.int32, (chunk, chunk), 1)
    same_block_causal = (ci <= ri) & ((ri - ci) <= (ri & (HG_BLOCK - 1)))
    tri = jnp.where(same_block_causal, 1.0, 0.0).astype(BF16)
    for c in range(tt // chunk):
        hi, mid, lo = _split3(logf[c * chunk:(c + 1) * chunk, :])
        lc = (jnp.dot(tri, hi, preferred_element_type=F32)
              + jnp.dot(tri, mid, preferred_element_type=F32)
              + jnp.dot(tri, lo, preferred_element_type=F32))
        lc_ref[c * chunk:(c + 1) * chunk, :] = lc

    ones = jnp.ones((hd, hd), BF16)
    row_id = lax.broadcasted_iota(jnp.int32, (HG_BLOCK, hd), 0)

    def block_body(blk, carry):
        r0 = pl.multiple_of(blk * HG_BLOCK, HG_BLOCK)
        for h in range(heads):
            cs = slice(h * hd, (h + 1) * hd)
            lc = lc_ref[pl.ds(r0, HG_BLOCK), cs]
            q = q_s[pl.ds(r0, HG_BLOCK), cs]
            k = k_s[pl.ds(r0, HG_BLOCK), cs]
            v = v_ref[pl.ds(r0, HG_BLOCK), cs]
            lc_last = lc[HG_BLOCK - 1:HG_BLOCK, :]
            q_dec = q * jnp.exp(lc)
            k_end = k * jnp.exp(lc_last - lc)
            g_blk = jnp.exp(lc_last)
            st = st_ref[h]
            o = lax.dot_general(q_dec.astype(BF16), st.astype(BF16), (((1,), (1,)), ((), ())),
                                preferred_element_type=F32)
            ws = []
            for s in range(HG_BLOCK):
                d = jnp.minimum(lc - lc[s:s + 1, :], 0.0)
                w = jnp.where(row_id >= s, q * k[s:s + 1, :] * jnp.exp(d), 0.0)
                ws.append(w.astype(BF16))
            sc = jnp.dot(jnp.concatenate(ws, axis=0), ones, preferred_element_type=F32)
            for s in range(HG_BLOCK):
                o = o + sc[s * HG_BLOCK:(s + 1) * HG_BLOCK, :] * v[s:s + 1, :]
            o_s[pl.ds(r0, HG_BLOCK), cs] = o
            d_st = lax.dot_general(v.astype(BF16), k_end.astype(BF16), (((0,), (0,)), ((), ())),
                                   preferred_element_type=F32)
            st_ref[h] = st * g_blk + d_st
        return carry

    lax.fori_loop(0, tt // HG_BLOCK, block_body, 0)

    for h in range(heads):
        cs = slice(h * hd, (h + 1) * hd)
        o = o_s[:, cs]
        ms = jnp.mean(o * o, axis=-1, keepdims=True)
        on = o * lax.rsqrt(ms + LN_EPS) * ng_ref[...]
        out_ref[:, cs] = (on * _silu(g_ref[:, cs])).astype(out_ref.dtype)

    @pl.when(t == nt - 1)
    def _():
        for h in range(heads):
            sout_ref[0, h] = st_ref[h].T


def _hgrn_branch(proj, s0, lbp, ng, *, nb_batch, seq, tt, col0):
    w = lbp.shape[1]
    heads = w // HG_HEAD_DIM
    nt = seq // tt
    chunk = min(tt, 128)
    col = lambda k: (lambda b, t: (b * nt + t, col0 + k))
    return pl.pallas_call(
        functools.partial(_hgrn_kernel, tt=tt, nt=nt, heads=heads, chunk=chunk),
        grid=(nb_batch, nt),
        in_specs=[
            pl.BlockSpec((tt, w), col(0)),
            pl.BlockSpec((tt, w), col(1)),
            pl.BlockSpec((tt, w), col(2)),
            pl.BlockSpec((tt, w), col(3)),
            pl.BlockSpec((1, heads, HG_HEAD_DIM, HG_HEAD_DIM), lambda b, t: (b, 0, 0, 0)),
            pl.BlockSpec(lbp.shape, lambda b, t: (0, 0)),
            pl.BlockSpec((1, HG_HEAD_DIM), lambda b, t: (0, 0)),
        ],
        out_specs=[
            pl.BlockSpec((tt, w), lambda b, t: (b * nt + t, 0)),
            pl.BlockSpec((1, heads, HG_HEAD_DIM, HG_HEAD_DIM), lambda b, t: (b, 0, 0, 0)),
        ],
        out_shape=[
            jax.ShapeDtypeStruct((nb_batch * seq, w), BF16),
            jax.ShapeDtypeStruct((nb_batch, heads, HG_HEAD_DIM, HG_HEAD_DIM), F32),
        ],
        scratch_shapes=[
            pltpu.VMEM((heads, HG_HEAD_DIM, HG_HEAD_DIM), F32),
            pltpu.VMEM((tt, w), F32),
            pltpu.VMEM((tt, w), F32),
            pltpu.VMEM((tt, w), F32),
            pltpu.VMEM((tt, w), F32),
        ],
        compiler_params=_params(("arbitrary", "arbitrary")),
        name="hgrn_branch",
    )(proj, proj, proj, proj, s0, lbp, ng)


def _out_proj_kernel(xp_ref, xs_ref, cp_ref, cs_ref, hp_ref, hs_ref, wc_ref, wh_ref, g_ref, b_ref,
                     o_ref, *, n_prompt_tiles, alpha):
    i = pl.program_id(0)

    def compute(x_ref, c_ref, h_ref):
        mixed = (jnp.dot(c_ref[...], wc_ref[...], preferred_element_type=F32)
                 + jnp.dot(h_ref[...], wh_ref[...], preferred_element_type=F32))
        return _layer_norm(alpha * x_ref[...] + mixed, g_ref[...], b_ref[...])

    @pl.when(i < n_prompt_tiles)
    def _():
        o_ref[...] = compute(xp_ref, cp_ref, hp_ref)

    @pl.when(i >= n_prompt_tiles)
    def _():
        o_ref[...] = compute(xs_ref, cs_ref, hs_ref)


def _out_proj(xp, xs, cp, cs, hp, hs, wc_bf, wh_bf, g, b, alpha, tm=256):
    mp, d = xp.shape
    ms = xs.shape[0]
    c = cp.shape[1]
    npt, nst = mp // tm, ms // tm
    p_idx = lambda i: (jnp.minimum(i, npt - 1), 0)
    s_idx = lambda i: (jnp.maximum(i - npt, 0), 0)
    return pl.pallas_call(
        functools.partial(_out_proj_kernel, n_prompt_tiles=npt, alpha=alpha),
        grid=(npt + nst,),
        in_specs=[
            pl.BlockSpec((tm, d), p_idx),
            pl.BlockSpec((tm, d), s_idx),
            pl.BlockSpec((tm, c), p_idx),
            pl.BlockSpec((tm, c), s_idx),
            pl.BlockSpec((tm, c), p_idx),
            pl.BlockSpec((tm, c), s_idx),
            pl.BlockSpec((c, d), lambda i: (0, 0)),
            pl.BlockSpec((c, d), lambda i: (0, 0)),
            pl.BlockSpec((1, d), lambda i: (0, 0)),
            pl.BlockSpec((1, d), lambda i: (0, 0)),
        ],
        out_specs=pl.BlockSpec((tm, d), lambda i: (i, 0)),
        out_shape=jax.ShapeDtypeStruct((mp + ms, d), F32),
        compiler_params=_params(("arbitrary",)),
        name="out_proj",
    )(xp, xs, cp, cs, hp, hs, wc_bf, wh_bf, g, b)


def _ffn_kernel(h_ref, wg_ref, wu_ref, wd_ref, g_ref, b_ref, yp_ref, ys_ref, hb_ref, acc_ref,
                *, n_prompt_tiles, n_ff_tiles, alpha):
    i = pl.program_id(0)
    j = pl.program_id(1)

    @pl.when(j == 0)
    def _():
        hb_ref[...] = h_ref[...].astype(BF16)

    hb = hb_ref[...]
    gate = jnp.dot(hb, wg_ref[...], preferred_element_type=F32)
    up = jnp.dot(hb, wu_ref[...], preferred_element_type=F32)
    act = (_silu(gate) * up).astype(BF16)
    part = jnp.dot(act, wd_ref[...], preferred_element_type=F32)

    @pl.when(j == 0)
    def _():
        acc_ref[...] = part

    @pl.when(j > 0)
    def _():
        acc_ref[...] += part

    def finish():
        return _layer_norm(alpha * h_ref[...] + acc_ref[...], g_ref[...], b_ref[...])

    @pl.when((j == n_ff_tiles - 1) & (i < n_prompt_tiles))
    def _():
        yp_ref[...] = finish()

    @pl.when((j == n_ff_tiles - 1) & (i >= n_prompt_tiles))
    def _():
        ys_ref[...] = finish()


def _ffn(h, wg_bf, wu_bf, wd_bf, g, b, alpha, mp, tf=512):
    m, d = h.shape
    ms = m - mp
    dff = wg_bf.shape[1]
    tm = ROW_TILE
    npt, nst = mp // tm, ms // tm
    nj = dff // tf
    return pl.pallas_call(
        functools.partial(_ffn_kernel, n_prompt_tiles=npt, n_ff_tiles=nj, alpha=alpha),
        grid=(npt + nst, nj),
        in_specs=[
            pl.BlockSpec((tm, d), lambda i, j: (i, 0)),
            pl.BlockSpec((d, tf), lambda i, j: (0, j)),
            pl.BlockSpec((d, tf), lambda i, j: (0, j)),
            pl.BlockSpec((tf, d), lambda i, j: (j, 0)),
            pl.BlockSpec((1, d), lambda i, j: (0, 0)),
            pl.BlockSpec((1, d), lambda i, j: (0, 0)),
        ],
        out_specs=[
            pl.BlockSpec((tm, d), lambda i, j: (jnp.minimum(i, npt - 1), 0)),
            pl.BlockSpec((tm, d), lambda i, j: (jnp.maximum(i - npt, 0), 0)),
        ],
        out_shape=[jax.ShapeDtypeStruct((mp, d), F32), jax.ShapeDtypeStruct((ms, d), F32)],
        scratch_shapes=[pltpu.VMEM((tm, d), BF16), pltpu.VMEM((tm, d), F32)],
        compiler_params=_params(("arbitrary", "arbitrary")),
        name="ffn",
    )(h, wg_bf, wu_bf, wd_bf, g, b)


def kernel(x_prompt, x_sample, cache_conv, state_hgrn, w_in, b_in, w_dw, b_dw, conv_norm_g, conv_norm_b,
           hg_lower_bounds, hg_norm_g, w_out, ln1_g, ln1_b, w_gate, w_up, w_down, ln2_g, ln2_b):
    depth = w_in.shape[0]
    assert depth == 1, "single-layer step"
    bp, seq, d = x_prompt.shape
    bs, dseq, _ = x_sample.shape
    conv_w = w_dw.shape[2]
    conv_state = w_dw.shape[1] - 1
    hg_w = hg_lower_bounds.shape[1]
    heads = hg_w // HG_HEAD_DIM
    assert conv_w == hg_w and w_in.shape[2] == 2 * conv_w + 4 * hg_w
    alpha = (2.0 * depth) ** 0.25

    xp = x_prompt.reshape(bp * seq, d)
    xs = x_sample.reshape(bs * dseq, d)
    row = lambda a: a.reshape(1, -1)

    proj_p, proj_s = _in_proj(xp, xs, w_in[0].astype(BF16), row(b_in[0]))

    pad = CONV_HALO - conv_state
    hist_p = jnp.zeros((bp, CONV_HALO, conv_w), F32)
    hist_s = jnp.pad(cache_conv[0], ((0, 0), (pad, 0), (0, 0)))
    conv_args = (w_dw[0], row(b_dw[0]), row(conv_norm_g[0]), row(conv_norm_b[0]))
    conv_p, tail_p = _conv_branch(proj_p, hist_p, *conv_args, nb_batch=bp, seq=seq, tt=256)
    conv_s, tail_s = _conv_branch(proj_s, hist_s, *conv_args, nb_batch=bs, seq=dseq, tt=dseq)

    s0_p = jnp.zeros((bp, heads, HG_HEAD_DIM, HG_HEAD_DIM), F32)
    hg_args = (hg_lower_bounds, row(hg_norm_g[0]))
    hg_p, st_p = _hgrn_branch(proj_p, s0_p, *hg_args, nb_batch=bp, seq=seq, tt=256, col0=2)
    hg_s, st_s = _hgrn_branch(proj_s, state_hgrn[0], *hg_args, nb_batch=bs, seq=dseq, tt=dseq, col0=2)

    w_out_bf = w_out[0].astype(BF16)
    h = _out_proj(xp, xs, conv_p, conv_s, hg_p, hg_s, w_out_bf[:conv_w], w_out_bf[conv_w:],
                  row(ln1_g[0]), row(ln1_b[0]), alpha)

    yp, ys = _ffn(h, w_gate[0].astype(BF16), w_up[0].astype(BF16), w_down[0].astype(BF16),
                  row(ln2_g[0]), row(ln2_b[0]), alpha, bp * seq)

    return (yp.reshape(bp, seq, d), ys.reshape(bs, dseq, d),
            tail_p[None, :, pad:, :], st_p[None].astype(x_prompt.dtype),
            tail_s[None, :, pad:, :].astype(cache_conv.dtype), st_s[None].astype(state_hgrn.dtype))
```

```python
import functools

import jax
import jax.numpy as jnp
from jax import lax
from jax.experimental import pallas as pl
from jax.experimental.pallas import tpu as pltpu

LN_EPS = 1e-5
LOG2E = 1.4426950408889634
HG_HEAD_DIM = 128
HG_BLOCK = 16
SUBLANES = 8
CONV_HALO = 32
VMEM_LIMIT = 56 * 1024 * 1024
ROW_TILE = 512

F32 = jnp.float32
BF16 = jnp.bfloat16


def _sigmoid(x):
    return 0.5 * jnp.tanh(0.5 * x) + 0.5


def _silu(x):
    return x * _sigmoid(x)


def _layer_norm(x, g, b):
    mu = jnp.mean(x, axis=-1, keepdims=True)
    xc = x - mu
    var = jnp.mean(xc * xc, axis=-1, keepdims=True)
    return xc * lax.rsqrt(var + LN_EPS) * g + b


def _params(sem):
    return pltpu.CompilerParams(dimension_semantics=sem, vmem_limit_bytes=VMEM_LIMIT)


def _in_proj_kernel(x_ref, w_ref, b_ref, o_ref, xb_ref):
    @pl.when(pl.program_id(1) == 0)
    def _():
        xb_ref[...] = x_ref[...].astype(BF16)

    o_ref[...] = jnp.dot(xb_ref[...], w_ref[...], preferred_element_type=F32) + b_ref[...]


def _in_proj(x, w_bf, b, tm, tn=1024):
    m, d = x.shape
    n = w_bf.shape[1]
    return pl.pallas_call(
        _in_proj_kernel,
        grid=(m // tm, n // tn),
        in_specs=[
            pl.BlockSpec((tm, d), lambda i, j: (i, 0)),
            pl.BlockSpec((d, tn), lambda i, j: (0, j)),
            pl.BlockSpec((1, tn), lambda i, j: (0, j)),
        ],
        out_specs=pl.BlockSpec((tm, tn), lambda i, j: (i, j)),
        out_shape=jax.ShapeDtypeStruct((m, n), F32),
        scratch_shapes=[pltpu.VMEM((tm, d), BF16)],
        compiler_params=_params(("arbitrary", "arbitrary")),
        name="in_proj",
    )(x, w_bf, b)


def _row_chunk(n, cap=64):
    best = SUBLANES
    for c in range(SUBLANES, cap + 1, SUBLANES):
        if n % c == 0:
            best = c
    return best


def _conv_kernel(a_ref, gt_ref, hist_ref, w_ref, b_ref, ng_ref, nb_ref, out_ref, tail_ref, sh_ref,
                 *, tt, nt, taps, rows_sub):
    t = pl.program_id(1)

    @pl.when(t == 0)
    def _():
        sh_ref[0, 0:CONV_HALO, :] = hist_ref[0]

    sh_ref[0, CONV_HALO:CONV_HALO + tt, :] = a_ref[...] * _sigmoid(gt_ref[...])

    n_shift = CONV_HALO + tt - SUBLANES
    chunk = _row_chunk(n_shift)
    for c0 in range(0, n_shift, chunk):
        window = sh_ref[0, c0:c0 + chunk + SUBLANES, :]
        for p in range(1, SUBLANES):
            sh_ref[p, c0:c0 + chunk, :] = pltpu.roll(window, chunk + SUBLANES - p, axis=0)[:chunk]

    lead = CONV_HALO - (taps - 1)
    n_ch = w_ref.shape[-1]
    for r in range(tt // rows_sub):
        base = r * rows_sub
        acc = None
        for j in range(taps):
            phase = (lead + j) % SUBLANES
            lo = base + (lead + j) - phase
            rows = sh_ref[phase, lo:lo + rows_sub, :].reshape(rows_sub // SUBLANES, SUBLANES, n_ch)
            term = rows * w_ref[j][None]
            acc = term if acc is None else acc + term
        hn = _layer_norm(acc.reshape(rows_sub, n_ch) + b_ref[...], ng_ref[...], nb_ref[...])
        out_ref[base:base + rows_sub, :] = _silu(hn).astype(out_ref.dtype)

    tail = sh_ref[0, tt:tt + CONV_HALO, :]

    @pl.when(t == nt - 1)
    def _():
        tail_ref[0] = tail

    if nt > 1:
        sh_ref[0, 0:CONV_HALO, :] = tail


def _conv_branch(proj, hist_pad, w_dw, b_dw, ng, nb, *, nb_batch, seq, tt):
    taps, c = w_dw.shape
    nt = seq // tt
    rows_sub = min(tt, 32)
    w_rep = jnp.broadcast_to(w_dw[:, None, :], (taps, SUBLANES, c))
    row_idx = lambda b, t: (b * nt + t, 0)
    return pl.pallas_call(
        functools.partial(_conv_kernel, tt=tt, nt=nt, taps=taps, rows_sub=rows_sub),
        grid=(nb_batch, nt),
        in_specs=[
            pl.BlockSpec((tt, c), row_idx),
            pl.BlockSpec((tt, c), lambda b, t: (b * nt + t, 1)),
            pl.BlockSpec((1, CONV_HALO, c), lambda b, t: (b, 0, 0)),
            pl.BlockSpec((taps, SUBLANES, c), lambda b, t: (0, 0, 0)),
            pl.BlockSpec((1, c), lambda b, t: (0, 0)),
            pl.BlockSpec((1, c), lambda b, t: (0, 0)),
            pl.BlockSpec((1, c), lambda b, t: (0, 0)),
        ],
        out_specs=[
            pl.BlockSpec((tt, c), row_idx),
            pl.BlockSpec((1, CONV_HALO, c), lambda b, t: (b, 0, 0)),
        ],
        out_shape=[
            jax.ShapeDtypeStruct((nb_batch * seq, c), BF16),
            jax.ShapeDtypeStruct((nb_batch, CONV_HALO, c), F32),
        ],
        scratch_shapes=[pltpu.VMEM((SUBLANES, CONV_HALO + tt, c), F32)],
        compiler_params=_params(("arbitrary", "arbitrary")),
        name="conv_branch",
    )(proj, proj, hist_pad, w_rep, b_dw, ng, nb)


def _split3(x):
    hi = x.astype(BF16)
    r1 = x - hi.astype(F32)
    mid = r1.astype(BF16)
    lo = (r1 - mid.astype(F32)).astype(BF16)
    return hi, mid, lo


def _hgrn_kernel(q_ref, f_ref, v_ref, g_ref, s0_ref, lbp_ref, ng_ref, out_ref, sout_ref,
                 st_ref, lc_s, lck_s, q_s, v_s, o_s, *, tt, nt, heads, chunk):
    t = pl.program_id(1)
    hd = HG_HEAD_DIM
    half = HG_BLOCK // 2

    @pl.when(t == 0)
    def _():
        for h in range(heads):
            st_ref[h] = s0_ref[0, h].T

    lbp = lbp_ref[...]
    e = jnp.exp(lbp - jnp.max(lbp, axis=0, keepdims=True))
    lb = e[0:1, :] / jnp.sum(e, axis=0, keepdims=True)

    def to_heads(dst, rows, val):
        for h in range(heads):
            dst[h, rows, :] = val[:, h * hd:(h + 1) * hd]

    f = lb + (1.0 - lb) * _sigmoid(f_ref[...])
    to_heads(q_s, slice(0, tt), _silu(q_ref[...]))
    to_heads(v_s, slice(0, tt), v_ref[...])
    log2f = jnp.log(f) * LOG2E
    log2k = jnp.log(1.0 - f) * LOG2E

    ri = lax.broadcasted_iota(jnp.int32, (chunk, chunk), 0)
    ci = lax.broadcasted_iota(jnp.int32, (chunk, chunk), 1)
    same_block_causal = (ci <= ri) & ((ri - ci) <= (ri & (HG_BLOCK - 1)))
    tri = jnp.where(same_block_causal, 1.0, 0.0).astype(BF16)
    for c in range(tt // chunk):
        rows = slice(c * chunk, (c + 1) * chunk)
        hi, mid, lo = _split3(log2f[rows, :])
        lc = (jnp.dot(tri, hi, preferred_element_type=F32)
              + jnp.dot(tri, mid, preferred_element_type=F32)
              + jnp.dot(tri, lo, preferred_element_type=F32))
        to_heads(lc_s, rows, lc)
        to_heads(lck_s, rows, lc - log2k[rows, :])

    ones = jnp.ones((hd, hd), BF16)
    row_id = lax.broadcasted_iota(jnp.int32, (half, hd), 0)

    def block_body(blk, carry):
        r0 = pl.multiple_of(blk * HG_BLOCK, HG_BLOCK)
        for h in range(heads):
            cs = slice(h * hd, (h + 1) * hd)
            lc = lc_s[h, pl.ds(r0, HG_BLOCK), :]
            lck = lck_s[h, pl.ds(r0, HG_BLOCK), :]
            q = q_s[h, pl.ds(r0, HG_BLOCK), :]
            v = v_s[h, pl.ds(r0, HG_BLOCK), :]
            lc_last = lc[HG_BLOCK - 1:HG_BLOCK, :]
            q_dec = q * jnp.exp2(lc)
            k_end = jnp.exp2(lc_last - lck)
            g_blk = jnp.exp2(lc_last)
            st = st_ref[h]
            o = lax.dot_general(q_dec.astype(BF16), st.astype(BF16), (((1,), (1,)), ((), ())),
                                preferred_element_type=F32)
            q_top, q_bot = q[:half], q[half:]
            lc_top, lc_bot = lc[:half], lc[half:]
            parts = []
            for s in range(HG_BLOCK):
                c = lck_s[h, pl.ds(r0 + s, 1), :]
                if s < half:
                    top = q_top * jnp.exp2(lc_top - c)
                    if s > 0:
                        top = jnp.where(row_id >= s, top, 0.0)
                    parts.append(top)
                    parts.append(q_bot * jnp.exp2(lc_bot - c))
                else:
                    bot = q_bot * jnp.exp2(lc_bot - c)
                    if s > half:
                        bot = jnp.where(row_id >= s - half, bot, 0.0)
                    parts.append(bot)
            sc = jnp.dot(jnp.concatenate(parts, axis=0).astype(BF16), ones, preferred_element_type=F32)
            o_top = None
            o_bot = None
            for s in range(HG_BLOCK):
                vs = v_s[h, pl.ds(r0 + s, 1), :]
                if s < half:
                    top = sc[2 * half * s:2 * half * s + half] * vs
                    bot = sc[2 * half * s + half:2 * half * (s + 1)] * vs
                    o_top = top if o_top is None else o_top + top
                else:
                    lo = 2 * half * half + half * (s - half)
                    bot = sc[lo:lo + half] * vs
                o_bot = bot if o_bot is None else o_bot + bot
            o_s[pl.ds(r0, HG_BLOCK), cs] = o + jnp.concatenate([o_top, o_bot], axis=0)
            d_st = lax.dot_general(v.astype(BF16), k_end.astype(BF16), (((0,), (0,)), ((), ())),
                                   preferred_element_type=F32)
            st_ref[h] = st * g_blk + d_st
        return carry

    lax.fori_loop(0, tt // HG_BLOCK, block_body, 0, unroll=min(4, tt // HG_BLOCK))

    for h in range(heads):
        cs = slice(h * hd, (h + 1) * hd)
        o = o_s[:, cs]
        ms = jnp.mean(o * o, axis=-1, keepdims=True)
        on = o * lax.rsqrt(ms + LN_EPS) * ng_ref[...]
        out_ref[:, cs] = (on * _silu(g_ref[:, cs])).astype(out_ref.dtype)

    @pl.when(t == nt - 1)
    def _():
        for h in range(heads):
            sout_ref[0, h] = st_ref[h].T


def _hgrn_branch(proj, s0, lbp, ng, *, nb_batch, seq, tt, col0):
    w = lbp.shape[1]
    heads = w // HG_HEAD_DIM
    nt = seq // tt
    chunk = min(tt, 128)
    col = lambda k: (lambda b, t: (b * nt + t, col0 + k))
    return pl.pallas_call(
        functools.partial(_hgrn_kernel, tt=tt, nt=nt, heads=heads, chunk=chunk),
        grid=(nb_batch, nt),
        in_specs=[
            pl.BlockSpec((tt, w), col(0)),
            pl.BlockSpec((tt, w), col(1)),
            pl.BlockSpec((tt, w), col(2)),
            pl.BlockSpec((tt, w), col(3)),
            pl.BlockSpec((1, heads, HG_HEAD_DIM, HG_HEAD_DIM), lambda b, t: (b, 0, 0, 0)),
            pl.BlockSpec(lbp.shape, lambda b, t: (0, 0)),
            pl.BlockSpec((1, HG_HEAD_DIM), lambda b, t: (0, 0)),
        ],
        out_specs=[
            pl.BlockSpec((tt, w), lambda b, t: (b * nt + t, 0)),
            pl.BlockSpec((1, heads, HG_HEAD_DIM, HG_HEAD_DIM), lambda b, t: (b, 0, 0, 0)),
        ],
        out_shape=[
            jax.ShapeDtypeStruct((nb_batch * seq, w), BF16),
            jax.ShapeDtypeStruct((nb_batch, heads, HG_HEAD_DIM, HG_HEAD_DIM), F32),
        ],
        scratch_shapes=[
            pltpu.VMEM((heads, HG_HEAD_DIM, HG_HEAD_DIM), F32),
            pltpu.VMEM((heads, tt, HG_HEAD_DIM), F32),
            pltpu.VMEM((heads, tt, HG_HEAD_DIM), F32),
            pltpu.VMEM((heads, tt, HG_HEAD_DIM), F32),
            pltpu.VMEM((heads, tt, HG_HEAD_DIM), F32),
            pltpu.VMEM((tt, w), F32),
        ],
        compiler_params=_params(("arbitrary", "arbitrary")),
        name="hgrn_branch",
    )(proj, proj, proj, proj, s0, lbp, ng)


def _out_proj_kernel(xp_ref, xs_ref, cp_ref, cs_ref, hp_ref, hs_ref, wc_ref, wh_ref, g_ref, b_ref,
                     o_ref, *, n_prompt_tiles, alpha):
    i = pl.program_id(0)

    def compute(x_ref, c_ref, h_ref):
        mixed = (jnp.dot(c_ref[...], wc_ref[...], preferred_element_type=F32)
                 + jnp.dot(h_ref[...], wh_ref[...], preferred_element_type=F32))
        return _layer_norm(alpha * x_ref[...] + mixed, g_ref[...], b_ref[...])

    @pl.when(i < n_prompt_tiles)
    def _():
        o_ref[...] = compute(xp_ref, cp_ref, hp_ref)

    @pl.when(i >= n_prompt_tiles)
    def _():
        o_ref[...] = compute(xs_ref, cs_ref, hs_ref)


def _out_proj(xp, xs, cp, cs, hp, hs, wc_bf, wh_bf, g, b, alpha, tm=256):
    mp, d = xp.shape
    ms = xs.shape[0]
    c = cp.shape[1]
    npt, nst = mp // tm, ms // tm
    p_idx = lambda i: (jnp.minimum(i, npt - 1), 0)
    s_idx = lambda i: (jnp.maximum(i - npt, 0), 0)
    return pl.pallas_call(
        functools.partial(_out_proj_kernel, n_prompt_tiles=npt, alpha=alpha),
        grid=(npt + nst,),
        in_specs=[
            pl.BlockSpec((tm, d), p_idx),
            pl.BlockSpec((tm, d), s_idx),
            pl.BlockSpec((tm, c), p_idx),
            pl.BlockSpec((tm, c), s_idx),
            pl.BlockSpec((tm, c), p_idx),
            pl.BlockSpec((tm, c), s_idx),
            pl.BlockSpec((c, d), lambda i: (0, 0)),
            pl.BlockSpec((c, d), lambda i: (0, 0)),
            pl.BlockSpec((1, d), lambda i: (0, 0)),
            pl.BlockSpec((1, d), lambda i: (0, 0)),
        ],
        out_specs=pl.BlockSpec((tm, d), lambda i: (i, 0)),
        out_shape=jax.ShapeDtypeStruct((mp + ms, d), F32),
        compiler_params=_params(("arbitrary",)),
        name="out_proj",
    )(xp, xs, cp, cs, hp, hs, wc_bf, wh_bf, g, b)


def _ffn_kernel(h_ref, wg_ref, wu_ref, wd_ref, g_ref, b_ref, yp_ref, ys_ref, hb_ref, acc_ref,
                *, n_prompt_tiles, n_ff_tiles, alpha):
    i = pl.program_id(0)
    j = pl.program_id(1)

    @pl.when(j == 0)
    def _():
        hb_ref[...] = h_ref[...].astype(BF16)
        acc_ref[...] = jnp.zeros_like(acc_ref)

    hb = hb_ref[...]
    gate = jnp.dot(hb, wg_ref[...], preferred_element_type=F32)
    up = jnp.dot(hb, wu_ref[...], preferred_element_type=F32)
    act = (_silu(gate) * up).astype(BF16)
    acc_ref[...] += jnp.dot(act, wd_ref[...], preferred_element_type=F32)

    def finish():
        return _layer_norm(alpha * h_ref[...] + acc_ref[...], g_ref[...], b_ref[...])

    @pl.when((j == n_ff_tiles - 1) & (i < n_prompt_tiles))
    def _():
        yp_ref[...] = finish()

    @pl.when((j == n_ff_tiles - 1) & (i >= n_prompt_tiles))
    def _():
        ys_ref[...] = finish()


def _ffn(h, wg_bf, wu_bf, wd_bf, g, b, alpha, mp, tf=512):
    m, d = h.shape
    ms = m - mp
    dff = wg_bf.shape[1]
    tm = ROW_TILE
    npt, nst = mp // tm, ms // tm
    nj = dff // tf
    return pl.pallas_call(
        functools.partial(_ffn_kernel, n_prompt_tiles=npt, n_ff_tiles=nj, alpha=alpha),
        grid=(npt + nst, nj),
        in_specs=[
            pl.BlockSpec((tm, d), lambda i, j: (i, 0)),
            pl.BlockSpec((d, tf), lambda i, j: (0, j)),
            pl.BlockSpec((d, tf), lambda i, j: (0, j)),
            pl.BlockSpec((tf, d), lambda i, j: (j, 0)),
            pl.BlockSpec((1, d), lambda i, j: (0, 0)),
            pl.BlockSpec((1, d), lambda i, j: (0, 0)),
        ],
        out_specs=[
            pl.BlockSpec((tm, d), lambda i, j: (jnp.minimum(i, npt - 1), 0)),
            pl.BlockSpec((tm, d), lambda i, j: (jnp.maximum(i - npt, 0), 0)),
        ],
        out_shape=[jax.ShapeDtypeStruct((mp, d), F32), jax.ShapeDtypeStruct((ms, d), F32)],
        scratch_shapes=[pltpu.VMEM((tm, d), BF16), pltpu.VMEM((tm, d), F32)],
        compiler_params=_params(("arbitrary", "arbitrary")),
        name="ffn",
    )(h, wg_bf, wu_bf, wd_bf, g, b)


def kernel(x_prompt, x_sample, cache_conv, state_hgrn, w_in, b_in, w_dw, b_dw, conv_norm_g, conv_norm_b,
           hg_lower_bounds, hg_norm_g, w_out, ln1_g, ln1_b, w_gate, w_up, w_down, ln2_g, ln2_b):
    depth = w_in.shape[0]
    assert depth == 1, "single-layer step"
    bp, seq, d = x_prompt.shape
    bs, dseq, _ = x_sample.shape
    conv_w = w_dw.shape[2]
    conv_state = w_dw.shape[1] - 1
    hg_w = hg_lower_bounds.shape[1]
    heads = hg_w // HG_HEAD_DIM
    assert conv_w == hg_w and w_in.shape[2] == 2 * conv_w + 4 * hg_w
    alpha = (2.0 * depth) ** 0.25

    xp = x_prompt.reshape(bp * seq, d)
    xs = x_sample.reshape(bs * dseq, d)
    row = lambda a: a.reshape(1, -1)

    w_in_bf = w_in[0].astype(BF16)
    proj_p = _in_proj(xp, w_in_bf, row(b_in[0]), tm=2 * ROW_TILE)
    proj_s = _in_proj(xs, w_in_bf, row(b_in[0]), tm=ROW_TILE)

    pad = CONV_HALO - conv_state
    hist_p = jnp.zeros((bp, CONV_HALO, conv_w), F32)
    hist_s = jnp.pad(cache_conv[0], ((0, 0), (pad, 0), (0, 0)))
    conv_args = (w_dw[0], row(b_dw[0]), row(conv_norm_g[0]), row(conv_norm_b[0]))
    conv_p, tail_p = _conv_branch(proj_p, hist_p, *conv_args, nb_batch=bp, seq=seq, tt=256)
    conv_s, tail_s = _conv_branch(proj_s, hist_s, *conv_args, nb_batch=bs, seq=dseq, tt=dseq)

    s0_p = jnp.zeros((bp, heads, HG_HEAD_DIM, HG_HEAD_DIM), F32)
    hg_args = (hg_lower_bounds, row(hg_norm_g[0]))
    hg_p, st_p = _hgrn_branch(proj_p, s0_p, *hg_args, nb_batch=bp, seq=seq, tt=256, col0=2)
    hg_s, st_s = _hgrn_branch(proj_s, state_hgrn[0], *hg_args, nb_batch=bs, seq=dseq, tt=dseq, col0=2)

    w_out_bf = w_out[0].astype(BF16)
    h = _out_proj(xp, xs, conv_p, conv_s, hg_p, hg_s, w_out_bf[:conv_w], w_out_bf[conv_w:],
                  row(ln1_g[0]), row(ln1_b[0]), alpha)

    yp, ys = _ffn(h, w_gate[0].astype(BF16), w_up[0].astype(BF16), w_down[0].astype(BF16),
                  row(ln2_g[0]), row(ln2_b[0]), alpha, bp * seq)

    return (yp.reshape(bp, seq, d), ys.reshape(bs, dseq, d),
            tail_p[None, :, pad:, :], st_p[None].astype(x_prompt.dtype),
            tail_s[None, :, pad:, :].astype(cache_conv.dtype), st_s[None].astype(state_hgrn.dtype))
```

```python
import functools

import jax
import jax.numpy as jnp
from jax import lax
from jax.experimental import pallas as pl
from jax.experimental.pallas import tpu as pltpu

LN_EPS = 1e-5
LOG2E = 1.4426950408889634
HG_HEAD_DIM = 128
HG_BLOCK = 16
SUBLANES = 8
CONV_HALO = 32
VMEM_LIMIT = 56 * 1024 * 1024
ROW_TILE = 512

F32 = jnp.float32
BF16 = jnp.bfloat16


def _sigmoid(x):
    return 0.5 * jnp.tanh(0.5 * x) + 0.5


def _silu(x):
    return x * _sigmoid(x)


def _layer_norm(x, g, b):
    mu = jnp.mean(x, axis=-1, keepdims=True)
    xc = x - mu
    var = jnp.mean(xc * xc, axis=-1, keepdims=True)
    return xc * lax.rsqrt(var + LN_EPS) * g + b


def _params(sem):
    return pltpu.CompilerParams(dimension_semantics=sem, vmem_limit_bytes=VMEM_LIMIT)


def _in_proj_kernel(x_ref, w_ref, b_ref, o_ref, *rest, emit_bf16):
    xb_ref = rest[-1]

    @pl.when(pl.program_id(1) == 0)
    def _():
        xb_ref[...] = x_ref[...].astype(BF16)

    w = w_ref[...].astype(BF16)
    if emit_bf16:
        rest[0][...] = w
    o_ref[...] = jnp.dot(xb_ref[...], w, preferred_element_type=F32) + b_ref[...]


def _in_proj(x, w, b, *, tm, tn, emit_bf16):
    m, d = x.shape
    n = w.shape[1]
    w_spec = pl.BlockSpec((d, tn), lambda i, j: (0, j))
    out_specs = [pl.BlockSpec((tm, tn), lambda i, j: (i, j))]
    out_shape = [jax.ShapeDtypeStruct((m, n), F32)]
    if emit_bf16:
        assert m == tm, "the bf16 weight copy is written once per column tile"
        out_specs.append(w_spec)
        out_shape.append(jax.ShapeDtypeStruct((d, n), BF16))
    return pl.pallas_call(
        functools.partial(_in_proj_kernel, emit_bf16=emit_bf16),
        grid=(m // tm, n // tn),
        in_specs=[pl.BlockSpec((tm, d), lambda i, j: (i, 0)), w_spec,
                  pl.BlockSpec((1, tn), lambda i, j: (0, j))],
        out_specs=out_specs,
        out_shape=out_shape,
        scratch_shapes=[pltpu.VMEM((tm, d), BF16)],
        compiler_params=_params(("arbitrary", "arbitrary")),
        name="in_proj",
    )(x, w, b)


def _row_chunk(n, cap=64):
    best = SUBLANES
    for c in range(SUBLANES, cap + 1, SUBLANES):
        if n % c == 0:
            best = c
    return best


def _conv_kernel(a_ref, gt_ref, hist_ref, w_ref, b_ref, ng_ref, nb_ref, out_ref, tail_ref, sh_ref,
                 *, tt, nt, taps, rows_sub):
    t = pl.program_id(1)

    @pl.when(t == 0)
    def _():
        sh_ref[0, 0:CONV_HALO, :] = hist_ref[0]

    sh_ref[0, CONV_HALO:CONV_HALO + tt, :] = a_ref[...] * _sigmoid(gt_ref[...])

    n_shift = CONV_HALO + tt - SUBLANES
    chunk = _row_chunk(n_shift)
    for c0 in range(0, n_shift, chunk):
        window = sh_ref[0, c0:c0 + chunk + SUBLANES, :]
        for p in range(1, SUBLANES):
            sh_ref[p, c0:c0 + chunk, :] = pltpu.roll(window, chunk + SUBLANES - p, axis=0)[:chunk]

    lead = CONV_HALO - (taps - 1)
    n_ch = w_ref.shape[-1]
    for r in range(tt // rows_sub):
        base = r * rows_sub
        acc = None
        for j in range(taps):
            phase = (lead + j) % SUBLANES
            lo = base + (lead + j) - phase
            rows = sh_ref[phase, lo:lo + rows_sub, :].reshape(rows_sub // SUBLANES, SUBLANES, n_ch)
            term = rows * w_ref[j][None]
            acc = term if acc is None else acc + term
        hn = _layer_norm(acc.reshape(rows_sub, n_ch) + b_ref[...], ng_ref[...], nb_ref[...])
        out_ref[base:base + rows_sub, :] = _silu(hn).astype(out_ref.dtype)

    tail = sh_ref[0, tt:tt + CONV_HALO, :]

    @pl.when(t == nt - 1)
    def _():
        tail_ref[0] = tail

    if nt > 1:
        sh_ref[0, 0:CONV_HALO, :] = tail


def _conv_branch(proj, hist_pad, w_dw, b_dw, ng, nb, *, nb_batch, seq, tt):
    taps, c = w_dw.shape
    nt = seq // tt
    rows_sub = min(tt, 32)
    w_rep = jnp.broadcast_to(w_dw[:, None, :], (taps, SUBLANES, c))
    row_idx = lambda b, t: (b * nt + t, 0)
    return pl.pallas_call(
        functools.partial(_conv_kernel, tt=tt, nt=nt, taps=taps, rows_sub=rows_sub),
        grid=(nb_batch, nt),
        in_specs=[
            pl.BlockSpec((tt, c), row_idx),
            pl.BlockSpec((tt, c), lambda b, t: (b * nt + t, 1)),
            pl.BlockSpec((1, CONV_HALO, c), lambda b, t: (b, 0, 0)),
            pl.BlockSpec((taps, SUBLANES, c), lambda b, t: (0, 0, 0)),
            pl.BlockSpec((1, c), lambda b, t: (0, 0)),
            pl.BlockSpec((1, c), lambda b, t: (0, 0)),
            pl.BlockSpec((1, c), lambda b, t: (0, 0)),
        ],
        out_specs=[
            pl.BlockSpec((tt, c), row_idx),
            pl.BlockSpec((1, CONV_HALO, c), lambda b, t: (b, 0, 0)),
        ],
        out_shape=[
            jax.ShapeDtypeStruct((nb_batch * seq, c), BF16),
            jax.ShapeDtypeStruct((nb_batch, CONV_HALO, c), F32),
        ],
        scratch_shapes=[pltpu.VMEM((SUBLANES, CONV_HALO + tt, c), F32)],
        compiler_params=_params(("arbitrary", "arbitrary")),
        name="conv_branch",
    )(proj, proj, hist_pad, w_rep, b_dw, ng, nb)


def _split3(x):
    hi = x.astype(BF16)
    r1 = x - hi.astype(F32)
    mid = r1.astype(BF16)
    lo = (r1 - mid.astype(F32)).astype(BF16)
    return hi, mid, lo


def _hgrn_kernel(q_ref, f_ref, v_ref, g_ref, s0_ref, lbp_ref, ng_ref, out_ref, sout_ref,
                 st_ref, lc_s, lck_s, q_s, v_s, o_s, *, tt, nt, heads, chunk, n_streams):
    t = pl.program_id(1)
    hd = HG_HEAD_DIM
    half = HG_BLOCK // 2

    @pl.when(t == 0)
    def _():
        for n in range(n_streams):
            for h in range(heads):
                st_ref[n * heads + h] = s0_ref[n, h].T

    lbp = lbp_ref[...]
    e = jnp.exp(lbp - jnp.max(lbp, axis=0, keepdims=True))
    lb = e[0:1, :] / jnp.sum(e, axis=0, keepdims=True)

    def to_heads(dst, rows, val):
        for h in range(heads):
            dst[h, rows, :] = val[:, h * hd:(h + 1) * hd]

    f = lb + (1.0 - lb) * _sigmoid(f_ref[...])
    to_heads(q_s, slice(0, tt), _silu(q_ref[...]))
    to_heads(v_s, slice(0, tt), v_ref[...])
    log2f = jnp.log(f) * LOG2E
    log2k = jnp.log(1.0 - f) * LOG2E

    ri = lax.broadcasted_iota(jnp.int32, (chunk, chunk), 0)
    ci = lax.broadcasted_iota(jnp.int32, (chunk, chunk), 1)
    same_block_causal = (ci <= ri) & ((ri - ci) <= (ri & (HG_BLOCK - 1)))
    tri = jnp.where(same_block_causal, 1.0, 0.0).astype(BF16)
    for c in range(tt // chunk):
        rows = slice(c * chunk, (c + 1) * chunk)
        hi, mid, lo = _split3(log2f[rows, :])
        lc = (jnp.dot(tri, hi, preferred_element_type=F32)
              + jnp.dot(tri, mid, preferred_element_type=F32)
              + jnp.dot(tri, lo, preferred_element_type=F32))
        to_heads(lc_s, rows, lc)
        to_heads(lck_s, rows, lc - log2k[rows, :])

    ones = jnp.ones((hd, hd), BF16)
    row_id = lax.broadcasted_iota(jnp.int32, (half, hd), 0)

    def block_body(blk, carry):
        r0 = pl.multiple_of(blk * HG_BLOCK, HG_BLOCK)
        st_base = blk * heads if n_streams > 1 else 0
        for h in range(heads):
            cs = slice(h * hd, (h + 1) * hd)
            lc = lc_s[h, pl.ds(r0, HG_BLOCK), :]
            lck = lck_s[h, pl.ds(r0, HG_BLOCK), :]
            q = q_s[h, pl.ds(r0, HG_BLOCK), :]
            v = v_s[h, pl.ds(r0, HG_BLOCK), :]
            lc_last = lc[HG_BLOCK - 1:HG_BLOCK, :]
            q_dec = q * jnp.exp2(lc)
            k_end = jnp.exp2(lc_last - lck)
            g_blk = jnp.exp2(lc_last)
            st = st_ref[st_base + h]
            o = lax.dot_general(q_dec.astype(BF16), st.astype(BF16), (((1,), (1,)), ((), ())),
                                preferred_element_type=F32)
            q_top, q_bot = q[:half], q[half:]
            lc_top, lc_bot = lc[:half], lc[half:]
            parts = []
            for s in range(HG_BLOCK):
                c = lck_s[h, pl.ds(r0 + s, 1), :]
                if s < half:
                    top = q_top * jnp.exp2(lc_top - c)
                    if s > 0:
                        top = jnp.where(row_id >= s, top, 0.0)
                    parts.append(top)
                    parts.append(q_bot * jnp.exp2(lc_bot - c))
                else:
                    bot = q_bot * jnp.exp2(lc_bot - c)
                    if s > half:
                        bot = jnp.where(row_id >= s - half, bot, 0.0)
                    parts.append(bot)
            sc = jnp.dot(jnp.concatenate(parts, axis=0).astype(BF16), ones, preferred_element_type=F32)
            o_top = None
            o_bot = None
            for s in range(HG_BLOCK):
                vs = v_s[h, pl.ds(r0 + s, 1), :]
                if s < half:
                    top = sc[2 * half * s:2 * half * s + half] * vs
                    bot = sc[2 * half * s + half:2 * half * (s + 1)] * vs
                    o_top = top if o_top is None else o_top + top
                else:
                    lo = 2 * half * half + half * (s - half)
                    bot = sc[lo:lo + half] * vs
                o_bot = bot if o_bot is None else o_bot + bot
            o_s[pl.ds(r0, HG_BLOCK), cs] = o + jnp.concatenate([o_top, o_bot], axis=0)
            d_st = lax.dot_general(v.astype(BF16), k_end.astype(BF16), (((0,), (0,)), ((), ())),
                                   preferred_element_type=F32)
            st_ref[st_base + h] = st * g_blk + d_st
        return carry

    lax.fori_loop(0, tt // HG_BLOCK, block_body, 0, unroll=min(4, tt // HG_BLOCK))

    for h in range(heads):
        cs = slice(h * hd, (h + 1) * hd)
        o = o_s[:, cs]
        ms = jnp.mean(o * o, axis=-1, keepdims=True)
        on = o * lax.rsqrt(ms + LN_EPS) * ng_ref[...]
        out_ref[:, cs] = (on * _silu(g_ref[:, cs])).astype(out_ref.dtype)

    @pl.when(t == nt - 1)
    def _():
        for n in range(n_streams):
            for h in range(heads):
                sout_ref[n, h] = st_ref[n * heads + h].T


def _hgrn_branch(proj, s0, lbp, ng, *, nb_batch, seq, tt, col0):
    w = lbp.shape[1]
    heads = w // HG_HEAD_DIM
    if seq == HG_BLOCK:
        n_streams = tt // seq
        n_groups, nt = nb_batch // n_streams, 1
    else:
        n_streams = 1
        n_groups, nt = nb_batch, seq // tt
    chunk = min(tt, 128)
    col = lambda k: (lambda g, t: (g * nt + t, col0 + k))
    state_spec = pl.BlockSpec((n_streams, heads, HG_HEAD_DIM, HG_HEAD_DIM), lambda g, t: (g, 0, 0, 0))
    slab = pltpu.VMEM((heads, tt, HG_HEAD_DIM), F32)
    return pl.pallas_call(
        functools.partial(_hgrn_kernel, tt=tt, nt=nt, heads=heads, chunk=chunk, n_streams=n_streams),
        grid=(n_groups, nt),
        in_specs=[
            pl.BlockSpec((tt, w), col(0)),
            pl.BlockSpec((tt, w), col(1)),
            pl.BlockSpec((tt, w), col(2)),
            pl.BlockSpec((tt, w), col(3)),
            state_spec,
            pl.BlockSpec(lbp.shape, lambda g, t: (0, 0)),
            pl.BlockSpec((1, HG_HEAD_DIM), lambda g, t: (0, 0)),
        ],
        out_specs=[
            pl.BlockSpec((tt, w), lambda g, t: (g * nt + t, 0)),
            state_spec,
        ],
        out_shape=[
            jax.ShapeDtypeStruct((nb_batch * seq, w), BF16),
            jax.ShapeDtypeStruct((nb_batch, heads, HG_HEAD_DIM, HG_HEAD_DIM), F32),
        ],
        scratch_shapes=[
            pltpu.VMEM((n_streams * heads, HG_HEAD_DIM, HG_HEAD_DIM), F32),
            slab, slab, slab, slab,
            pltpu.VMEM((tt, w), F32),
        ],
        compiler_params=_params(("arbitrary", "arbitrary")),
        name="hgrn_branch",
    )(proj, proj, proj, proj, s0, lbp, ng)


def _out_proj_kernel(x_ref, c_ref, h_ref, wc_ref, wh_ref, g_ref, b_ref, o_ref, *, alpha, rows_sub):
    for r0 in range(0, x_ref.shape[0], rows_sub):
        rows = slice(r0, r0 + rows_sub)
        mixed = (jnp.dot(c_ref[rows, :], wc_ref[...], preferred_element_type=F32)
                 + jnp.dot(h_ref[rows, :], wh_ref[...], preferred_element_type=F32))
        o_ref[rows, :] = _layer_norm(alpha * x_ref[rows, :] + mixed, g_ref[...], b_ref[...])


def _out_proj(x, conv, hg, wc_bf, wh_bf, g, b, alpha, *, tm, rows_sub=128):
    m, d = x.shape
    c = conv.shape[1]
    row = lambda i: (i, 0)
    fixed = lambda i: (0, 0)
    return pl.pallas_call(
        functools.partial(_out_proj_kernel, alpha=alpha, rows_sub=rows_sub),
        grid=(m // tm,),
        in_specs=[
            pl.BlockSpec((tm, d), row),
            pl.BlockSpec((tm, c), row),
            pl.BlockSpec((tm, c), row),
            pl.BlockSpec((c, d), fixed),
            pl.BlockSpec((c, d), fixed),
            pl.BlockSpec((1, d), fixed),
            pl.BlockSpec((1, d), fixed),
        ],
        out_specs=pl.BlockSpec((tm, d), row),
        out_shape=jax.ShapeDtypeStruct((m, d), F32),
        compiler_params=_params(("arbitrary",)),
        name="out_proj",
    )(x, conv, hg, wc_bf, wh_bf, g, b)


def _ffn_kernel(h_ref, wg_ref, wu_ref, wd_ref, g_ref, b_ref, y_ref, *rest, n_ff_tiles, alpha, emit_bf16):
    hb_ref = rest[-1]
    j = pl.program_id(1)

    @pl.when(j == 0)
    def _():
        hb_ref[...] = h_ref[...].astype(BF16)
        y_ref[...] = jnp.zeros_like(y_ref)

    wg = wg_ref[...].astype(BF16)
    wu = wu_ref[...].astype(BF16)
    wd = wd_ref[...].astype(BF16)
    if emit_bf16:
        rest[0][...] = wg
        rest[1][...] = wu
        rest[2][...] = wd

    hb = hb_ref[...]
    gate = jnp.dot(hb, wg, preferred_element_type=F32)
    up = jnp.dot(hb, wu, preferred_element_type=F32)
    act = (_silu(gate) * up).astype(BF16)
    y_ref[...] += jnp.dot(act, wd, preferred_element_type=F32)

    @pl.when(j == n_ff_tiles - 1)
    def _():
        y_ref[...] = _layer_norm(alpha * h_ref[...] + y_ref[...], g_ref[...], b_ref[...])


def _ffn(h, wg, wu, wd, g, b, alpha, *, tm, tf, emit_bf16):
    m, d = h.shape
    dff = wg.shape[1]
    nj = dff // tf
    up_spec = pl.BlockSpec((d, tf), lambda i, j: (0, j))
    down_spec = pl.BlockSpec((tf, d), lambda i, j: (j, 0))
    fixed = lambda i, j: (0, 0)
    out_specs = [pl.BlockSpec((tm, d), lambda i, j: (i, 0))]
    out_shape = [jax.ShapeDtypeStruct((m, d), F32)]
    if emit_bf16:
        assert m == tm, "the bf16 weight copies are written once per d_ff tile"
        out_specs += [up_spec, up_spec, down_spec]
        out_shape += [jax.ShapeDtypeStruct((d, dff), BF16), jax.ShapeDtypeStruct((d, dff), BF16),
                      jax.ShapeDtypeStruct((dff, d), BF16)]
    return pl.pallas_call(
        functools.partial(_ffn_kernel, n_ff_tiles=nj, alpha=alpha, emit_bf16=emit_bf16),
        grid=(m // tm, nj),
        in_specs=[
            pl.BlockSpec((tm, d), lambda i, j: (i, 0)),
            up_spec, up_spec, down_spec,
            pl.BlockSpec((1, d), fixed),
            pl.BlockSpec((1, d), fixed),
        ],
        out_specs=out_specs,
        out_shape=out_shape,
        scratch_shapes=[pltpu.VMEM((tm, d), BF16)],
        compiler_params=_params(("arbitrary", "arbitrary")),
        name="ffn",
    )(h, wg, wu, wd, g, b)


def kernel(x_prompt, x_sample, cache_conv, state_hgrn, w_in, b_in, w_dw, b_dw, conv_norm_g, conv_norm_b,
           hg_lower_bounds, hg_norm_g, w_out, ln1_g, ln1_b, w_gate, w_up, w_down, ln2_g, ln2_b):
    depth = w_in.shape[0]
    assert depth == 1, "single-layer step"
    bp, seq, d = x_prompt.shape
    bs, dseq, _ = x_sample.shape
    conv_w = w_dw.shape[2]
    conv_state = w_dw.shape[1] - 1
    hg_w = hg_lower_bounds.shape[1]
    heads = hg_w // HG_HEAD_DIM
    assert conv_w == hg_w and w_in.shape[2] == 2 * conv_w + 4 * hg_w
    assert bs * dseq == ROW_TILE, "the sample rows form one dense-stage row tile"
    alpha = (2.0 * depth) ** 0.25

    xp = x_prompt.reshape(bp * seq, d)
    xs = x_sample.reshape(bs * dseq, d)
    row = lambda a: a.reshape(1, -1)

    proj_s, w_in_bf = _in_proj(xs, w_in[0], row(b_in[0]), tm=ROW_TILE, tn=512, emit_bf16=True)
    (proj_p,) = _in_proj(xp, w_in_bf, row(b_in[0]), tm=2 * ROW_TILE, tn=1024, emit_bf16=False)

    pad = CONV_HALO - conv_state
    hist_p = jnp.zeros((bp, CONV_HALO, conv_w), F32)
    hist_s = jnp.pad(cache_conv[0], ((0, 0), (pad, 0), (0, 0)))
    conv_args = (w_dw[0], row(b_dw[0]), row(conv_norm_g[0]), row(conv_norm_b[0]))
    conv_s, tail_s = _conv_branch(proj_s, hist_s, *conv_args, nb_batch=bs, seq=dseq, tt=dseq)
    conv_p, tail_p = _conv_branch(proj_p, hist_p, *conv_args, nb_batch=bp, seq=seq, tt=256)

    s0_p = jnp.zeros((bp, heads, HG_HEAD_DIM, HG_HEAD_DIM), F32)
    hg_args = (hg_lower_bounds, row(hg_norm_g[0]))
    hg_s, st_s = _hgrn_branch(proj_s, state_hgrn[0], *hg_args, nb_batch=bs, seq=dseq, tt=8 * dseq, col0=2)
    hg_p, st_p = _hgrn_branch(proj_p, s0_p, *hg_args, nb_batch=bp, seq=seq, tt=256, col0=2)

    w_out_bf = w_out[0].astype(BF16)
    out_args = (w_out_bf[:conv_w], w_out_bf[conv_w:], row(ln1_g[0]), row(ln1_b[0]), alpha)
    h_s = _out_proj(xs, conv_s, hg_s, *out_args, tm=ROW_TILE)
    h_p = _out_proj(xp, conv_p, hg_p, *out_args, tm=ROW_TILE)

    ln2 = (row(ln2_g[0]), row(ln2_b[0]), alpha)
    ys, wg_bf, wu_bf, wd_bf = _ffn(h_s, w_gate[0], w_up[0], w_down[0], *ln2,
                                   tm=ROW_TILE, tf=256, emit_bf16=True)
    (yp,) = _ffn(h_p, wg_bf, wu_bf, wd_bf, *ln2, tm=ROW_TILE, tf=512, emit_bf16=False)

    return (yp.reshape(bp, seq, d), ys.reshape(bs, dseq, d),
            tail_p[None, :, pad:, :], st_p[None].astype(x_prompt.dtype),
            tail_s[None, :, pad:, :].astype(cache_conv.dtype), st_s[None].astype(state_hgrn.dtype))
```

```python
import functools

import jax
import jax.numpy as jnp
from jax import lax
from jax.experimental import pallas as pl
from jax.experimental.pallas import tpu as pltpu

LN_EPS = 1e-5
LOG2E = 1.4426950408889634
HG_HEAD_DIM = 128
HG_BLOCK = 16
SUBLANES = 8
CONV_HALO = 32
VMEM_LIMIT = 56 * 1024 * 1024
ROW_TILE = 512

F32 = jnp.float32
BF16 = jnp.bfloat16


def _sigmoid(x):
    return 0.5 * jnp.tanh(0.5 * x) + 0.5


def _silu(x):
    hx = 0.5 * x
    return hx * jnp.tanh(hx) + hx


def _layer_norm(x, g, b):
    mu = jnp.mean(x, axis=-1, keepdims=True)
    xc = x - mu
    var = jnp.mean(xc * xc, axis=-1, keepdims=True)
    return xc * lax.rsqrt(var + LN_EPS) * g + b


def _params(sem):
    return pltpu.CompilerParams(dimension_semantics=sem, vmem_limit_bytes=VMEM_LIMIT)


def _in_proj_kernel(x_ref, w_ref, b_ref, o_ref, *rest, emit_bf16):
    xb_ref = rest[-1]

    @pl.when(pl.program_id(1) == 0)
    def _():
        xb_ref[...] = x_ref[...].astype(BF16)

    w = w_ref[...].astype(BF16)
    if emit_bf16:
        rest[0][...] = w
    o_ref[...] = jnp.dot(xb_ref[...], w, preferred_element_type=F32) + b_ref[...]


def _in_proj(x, w, b, *, tm, tn, emit_bf16):
    m, d = x.shape
    n = w.shape[1]
    w_spec = pl.BlockSpec((d, tn), lambda i, j: (0, j))
    out_specs = [pl.BlockSpec((tm, tn), lambda i, j: (i, j))]
    out_shape = [jax.ShapeDtypeStruct((m, n), F32)]
    if emit_bf16:
        assert m == tm, "the bf16 weight copy is written once per column tile"
        out_specs.append(w_spec)
        out_shape.append(jax.ShapeDtypeStruct((d, n), BF16))
    return pl.pallas_call(
        functools.partial(_in_proj_kernel, emit_bf16=emit_bf16),
        grid=(m // tm, n // tn),
        in_specs=[pl.BlockSpec((tm, d), lambda i, j: (i, 0)), w_spec,
                  pl.BlockSpec((1, tn), lambda i, j: (0, j))],
        out_specs=out_specs,
        out_shape=out_shape,
        scratch_shapes=[pltpu.VMEM((tm, d), BF16)],
        compiler_params=_params(("arbitrary", "arbitrary")),
        name="in_proj",
    )(x, w, b)


SHIFT_CHUNK = 40


def _conv_kernel(a_ref, gt_ref, hist_ref, w_ref, b_ref, ng_ref, nb_ref, *rest, tt, nt, taps, rows_sub, n_cast):
    cast_in = rest[:n_cast]
    out_ref, tail_ref = rest[n_cast:n_cast + 2]
    cast_out = rest[n_cast + 2:2 * n_cast + 2]
    sh_ref = rest[-1]
    t = pl.program_id(1)
    lead = CONV_HALO - (taps - 1)
    n_ch = w_ref.shape[-1]

    for src, dst in zip(cast_in, cast_out):
        dst[...] = src[...].astype(BF16)

    @pl.when(t == 0)
    def _():
        sh_ref[0, 0:SUBLANES, :] = jnp.zeros((SUBLANES, n_ch), F32)
        sh_ref[0, lead:CONV_HALO, :] = hist_ref[0]

    sh_ref[0, CONV_HALO:CONV_HALO + tt, :] = a_ref[...] * _sigmoid(gt_ref[...])

    n_shift = CONV_HALO + tt - SUBLANES
    for c0 in range(0, n_shift, SHIFT_CHUNK):
        chunk = min(SHIFT_CHUNK, n_shift - c0)
        window = sh_ref[0, c0:c0 + chunk + SUBLANES, :]
        for p in range(1, SUBLANES):
            sh_ref[p, c0:c0 + chunk, :] = pltpu.roll(window, chunk + SUBLANES - p, axis=0)[:chunk]

    for base in range(0, tt, rows_sub):
        acc = None
        for j in range(taps):
            phase = (lead + j) % SUBLANES
            lo = base + (lead + j) - phase
            rows = sh_ref[phase, lo:lo + rows_sub, :].reshape(rows_sub // SUBLANES, SUBLANES, n_ch)
            term = rows * w_ref[j][None]
            acc = term if acc is None else acc + term
        hn = _layer_norm(acc.reshape(rows_sub, n_ch) + b_ref[...], ng_ref[...], nb_ref[...])
        out_ref[base:base + rows_sub, :] = _silu(hn).astype(out_ref.dtype)

    @pl.when(t == nt - 1)
    def _():
        tail_ref[0] = sh_ref[0, tt + lead:tt + CONV_HALO, :]

    if nt > 1:
        sh_ref[0, 0:CONV_HALO, :] = sh_ref[0, tt:tt + CONV_HALO, :]


def _conv_branch(proj, hist, w_dw, b_dw, ng, nb, *, nb_batch, seq, tt, cast=()):
    taps, c = w_dw.shape
    nt = seq // tt
    n_steps = nb_batch * nt
    rows_sub = min(tt, 32)
    w_rep = jnp.broadcast_to(w_dw[:, None, :], (taps, SUBLANES, c))
    row_idx = lambda b, t: (b * nt + t, 0)
    fixed = lambda b, t: (0, 0)
    seq_blk = pl.BlockSpec((1, taps - 1, c), lambda b, t: (b, 0, 0))
    cast_specs = [pl.BlockSpec((w.shape[0] // n_steps, w.shape[1]), row_idx) for w in cast]
    assert all(w.shape[0] % (n_steps * 2 * SUBLANES) == 0 for w in cast)
    outs = pl.pallas_call(
        functools.partial(_conv_kernel, tt=tt, nt=nt, taps=taps, rows_sub=rows_sub, n_cast=len(cast)),
        grid=(nb_batch, nt),
        in_specs=[
            pl.BlockSpec((tt, c), row_idx),
            pl.BlockSpec((tt, c), lambda b, t: (b * nt + t, 1)),
            seq_blk,
            pl.BlockSpec((taps, SUBLANES, c), lambda b, t: (0, 0, 0)),
            pl.BlockSpec((1, c), fixed),
            pl.BlockSpec((1, c), fixed),
            pl.BlockSpec((1, c), fixed),
        ] + cast_specs,
        out_specs=[pl.BlockSpec((tt, c), row_idx), seq_blk] + cast_specs,
        out_shape=[
            jax.ShapeDtypeStruct((nb_batch * seq, c), BF16),
            jax.ShapeDtypeStruct((nb_batch, taps - 1, c), F32),
        ] + [jax.ShapeDtypeStruct(w.shape, BF16) for w in cast],
        scratch_shapes=[pltpu.VMEM((SUBLANES, CONV_HALO + tt, c), F32)],
        compiler_params=_params(("arbitrary", "arbitrary")),
        name="conv_branch",
    )(proj, proj, hist, w_rep, b_dw, ng, nb, *cast)
    return outs[0], outs[1], outs[2:]


def _split3(x):
    hi = x.astype(BF16)
    r1 = x - hi.astype(F32)
    mid = r1.astype(BF16)
    lo = (r1 - mid.astype(F32)).astype(BF16)
    return hi, mid, lo


def _hgrn_kernel(q_ref, f_ref, v_ref, g_ref, s0_ref, lbp_ref, ng_ref, out_ref, sout_ref,
                 st_ref, lc_s, lck_s, q_s, v_s, o_s, *, tt, nt, heads, chunk, n_streams):
    t = pl.program_id(1)
    hd = HG_HEAD_DIM
    half = HG_BLOCK // 2

    @pl.when(t == 0)
    def _():
        for n in range(n_streams):
            for h in range(heads):
                st_ref[n * heads + h] = s0_ref[n, h].T

    lbp = lbp_ref[...]
    e = jnp.exp(lbp - jnp.max(lbp, axis=0, keepdims=True))
    lb = e[0:1, :] / jnp.sum(e, axis=0, keepdims=True)

    def to_heads(dst, rows, val):
        for h in range(heads):
            dst[h, rows, :] = val[:, h * hd:(h + 1) * hd]

    k = (0.5 * (1.0 - lb)) * (1.0 - jnp.tanh(0.5 * f_ref[...]))
    log2f = jnp.log(1.0 - k) * LOG2E
    log2k = jnp.log(k) * LOG2E
    to_heads(q_s, slice(0, tt), _silu(q_ref[...]))
    to_heads(v_s, slice(0, tt), v_ref[...])

    ri = lax.broadcasted_iota(jnp.int32, (chunk, chunk), 0)
    ci = lax.broadcasted_iota(jnp.int32, (chunk, chunk), 1)
    same_block_causal = (ci <= ri) & ((ri - ci) <= (ri & (HG_BLOCK - 1)))
    tri = jnp.where(same_block_causal, 1.0, 0.0).astype(BF16)
    for c in range(tt // chunk):
        rows = slice(c * chunk, (c + 1) * chunk)
        hi, mid, lo = _split3(log2f[rows, :])
        lc = (jnp.dot(tri, hi, preferred_element_type=F32)
              + jnp.dot(tri, mid, preferred_element_type=F32)
              + jnp.dot(tri, lo, preferred_element_type=F32))
        to_heads(lc_s, rows, lc)
        to_heads(lck_s, rows, lc - log2k[rows, :])

    ones = jnp.ones((hd, hd), BF16)
    row_id = lax.broadcasted_iota(jnp.int32, (half, hd), 0)

    def block_body(blk, carry):
        r0 = pl.multiple_of(blk * HG_BLOCK, HG_BLOCK)
        st_base = blk * heads if n_streams > 1 else 0
        for h in range(heads):
            cs = slice(h * hd, (h + 1) * hd)
            lc = lc_s[h, pl.ds(r0, HG_BLOCK), :]
            lck = lck_s[h, pl.ds(r0, HG_BLOCK), :]
            q = q_s[h, pl.ds(r0, HG_BLOCK), :]
            v = v_s[h, pl.ds(r0, HG_BLOCK), :]
            lc_last = lc[HG_BLOCK - 1:HG_BLOCK, :]
            q_dec = q * jnp.exp2(lc)
            k_end = jnp.exp2(lc_last - lck)
            g_blk = jnp.exp2(lc_last)
            st = st_ref[st_base + h]
            o = lax.dot_general(q_dec.astype(BF16), st.astype(BF16), (((1,), (1,)), ((), ())),
                                preferred_element_type=F32)
            q_top, q_bot = q[:half], q[half:]
            lc_top, lc_bot = lc[:half], lc[half:]
            parts = []
            for s in range(HG_BLOCK):
                c = lck_s[h, pl.ds(r0 + s, 1), :]
                if s < half:
                    top = q_top * jnp.exp2(lc_top - c)
                    if s > 0:
                        top = jnp.where(row_id >= s, top, 0.0)
                    parts.append(top)
                    parts.append(q_bot * jnp.exp2(lc_bot - c))
                else:
                    bot = q_bot * jnp.exp2(lc_bot - c)
                    if s > half:
                        bot = jnp.where(row_id >= s - half, bot, 0.0)
                    parts.append(bot)
            sc = jnp.dot(jnp.concatenate(parts, axis=0).astype(BF16), ones, preferred_element_type=F32)
            o_top = None
            o_bot = None
            for s in range(HG_BLOCK):
                vs = v_s[h, pl.ds(r0 + s, 1), :]
                if s < half:
                    top = sc[2 * half * s:2 * half * s + half] * vs
                    bot = sc[2 * half * s + half:2 * half * (s + 1)] * vs
                    o_top = top if o_top is None else o_top + top
                else:
                    lo = 2 * half * half + half * (s - half)
                    bot = sc[lo:lo + half] * vs
                o_bot = bot if o_bot is None else o_bot + bot
            o_s[pl.ds(r0, HG_BLOCK), cs] = o + jnp.concatenate([o_top, o_bot], axis=0)
            d_st = lax.dot_general(v.astype(BF16), k_end.astype(BF16), (((0,), (0,)), ((), ())),
                                   preferred_element_type=F32)
            st_ref[st_base + h] = st * g_blk + d_st
        return carry

    lax.fori_loop(0, tt // HG_BLOCK, block_body, 0, unroll=min(4, tt // HG_BLOCK))

    for h in range(heads):
        cs = slice(h * hd, (h + 1) * hd)
        o = o_s[:, cs]
        ms = jnp.mean(o * o, axis=-1, keepdims=True)
        on = o * lax.rsqrt(ms + LN_EPS) * ng_ref[...]
        out_ref[:, cs] = (on * _silu(g_ref[:, cs])).astype(out_ref.dtype)

    @pl.when(t == nt - 1)
    def _():
        for n in range(n_streams):
            for h in range(heads):
                sout_ref[n, h] = st_ref[n * heads + h].T


def _hgrn_branch(proj, s0, lbp, ng, *, nb_batch, seq, tt, col0):
    w = lbp.shape[1]
    heads = w // HG_HEAD_DIM
    if seq == HG_BLOCK:
        n_streams = tt // seq
        n_groups, nt = nb_batch // n_streams, 1
    else:
        n_streams = 1
        n_groups, nt = nb_batch, seq // tt
    chunk = min(tt, 128)
    col = lambda k: (lambda g, t: (g * nt + t, col0 + k))
    state_spec = pl.BlockSpec((n_streams, heads, HG_HEAD_DIM, HG_HEAD_DIM), lambda g, t: (g, 0, 0, 0))
    slab = pltpu.VMEM((heads, tt, HG_HEAD_DIM), F32)
    return pl.pallas_call(
        functools.partial(_hgrn_kernel, tt=tt, nt=nt, heads=heads, chunk=chunk, n_streams=n_streams),
        grid=(n_groups, nt),
        in_specs=[
            pl.BlockSpec((tt, w), col(0)),
            pl.BlockSpec((tt, w), col(1)),
            pl.BlockSpec((tt, w), col(2)),
            pl.BlockSpec((tt, w), col(3)),
            state_spec,
            pl.BlockSpec(lbp.shape, lambda g, t: (0, 0)),
            pl.BlockSpec((1, HG_HEAD_DIM), lambda g, t: (0, 0)),
        ],
        out_specs=[
            pl.BlockSpec((tt, w), lambda g, t: (g * nt + t, 0)),
            state_spec,
        ],
        out_shape=[
            jax.ShapeDtypeStruct((nb_batch * seq, w), BF16),
            jax.ShapeDtypeStruct((nb_batch, heads, HG_HEAD_DIM, HG_HEAD_DIM), F32),
        ],
        scratch_shapes=[
            pltpu.VMEM((n_streams * heads, HG_HEAD_DIM, HG_HEAD_DIM), F32),
            slab, slab, slab, slab,
            pltpu.VMEM((tt, w), F32),
        ],
        compiler_params=_params(("arbitrary", "arbitrary")),
        name="hgrn_branch",
    )(proj, proj, proj, proj, s0, lbp, ng)


def _out_proj_kernel(x_ref, c_ref, h_ref, wc_ref, wh_ref, g_ref, b_ref, o_ref, *, alpha, rows_sub):
    for r0 in range(0, x_ref.shape[0], rows_sub):
        rows = slice(r0, r0 + rows_sub)
        mixed = (jnp.dot(c_ref[rows, :], wc_ref[...], preferred_element_type=F32)
                 + jnp.dot(h_ref[rows, :], wh_ref[...], preferred_element_type=F32))
        o_ref[rows, :] = _layer_norm(alpha * x_ref[rows, :] + mixed, g_ref[...], b_ref[...])


def _out_proj(x, conv, hg, w_bf, g, b, alpha, *, tm, rows_sub=128):
    m, d = x.shape
    c = conv.shape[1]
    row = lambda i: (i, 0)
    fixed = lambda i: (0, 0)
    return pl.pallas_call(
        functools.partial(_out_proj_kernel, alpha=alpha, rows_sub=rows_sub),
        grid=(m // tm,),
        in_specs=[
            pl.BlockSpec((tm, d), row),
            pl.BlockSpec((tm, c), row),
            pl.BlockSpec((tm, c), row),
            pl.BlockSpec((c, d), fixed),
            pl.BlockSpec((c, d), lambda i: (1, 0)),
            pl.BlockSpec((1, d), fixed),
            pl.BlockSpec((1, d), fixed),
        ],
        out_specs=pl.BlockSpec((tm, d), row),
        out_shape=jax.ShapeDtypeStruct((m, d), F32),
        compiler_params=_params(("arbitrary",)),
        name="out_proj",
    )(x, conv, hg, w_bf, w_bf, g, b)


def _ffn_kernel(h_ref, wg_ref, wu_ref, wd_ref, g_ref, b_ref, y_ref, hb_ref, *, n_ff_tiles, alpha):
    j = pl.program_id(1)

    @pl.when(j == 0)
    def _():
        hb_ref[...] = h_ref[...].astype(BF16)
        y_ref[...] = jnp.zeros_like(y_ref)

    hb = hb_ref[...]
    gate = jnp.dot(hb, wg_ref[...], preferred_element_type=F32)
    up = jnp.dot(hb, wu_ref[...], preferred_element_type=F32)
    act = (_silu(gate) * up).astype(BF16)
    y_ref[...] += jnp.dot(act, wd_ref[...], preferred_element_type=F32)

    @pl.when(j == n_ff_tiles - 1)
    def _():
        y_ref[...] = _layer_norm(alpha * h_ref[...] + y_ref[...], g_ref[...], b_ref[...])


def _ffn(h, wg_bf, wu_bf, wd_bf, g, b, alpha, *, tm, tf):
    m, d = h.shape
    dff = wg_bf.shape[1]
    nj = dff // tf
    up_spec = pl.BlockSpec((d, tf), lambda i, j: (0, j))
    fixed = lambda i, j: (0, 0)
    return pl.pallas_call(
        functools.partial(_ffn_kernel, n_ff_tiles=nj, alpha=alpha),
        grid=(m // tm, nj),
        in_specs=[
            pl.BlockSpec((tm, d), lambda i, j: (i, 0)),
            up_spec, up_spec,
            pl.BlockSpec((tf, d), lambda i, j: (j, 0)),
            pl.BlockSpec((1, d), fixed),
            pl.BlockSpec((1, d), fixed),
        ],
        out_specs=pl.BlockSpec((tm, d), lambda i, j: (i, 0)),
        out_shape=jax.ShapeDtypeStruct((m, d), F32),
        scratch_shapes=[pltpu.VMEM((tm, d), BF16)],
        compiler_params=_params(("arbitrary", "arbitrary")),
        name="ffn",
    )(h, wg_bf, wu_bf, wd_bf, g, b)


def kernel(x_prompt, x_sample, cache_conv, state_hgrn, w_in, b_in, w_dw, b_dw, conv_norm_g, conv_norm_b,
           hg_lower_bounds, hg_norm_g, w_out, ln1_g, ln1_b, w_gate, w_up, w_down, ln2_g, ln2_b):
    depth = w_in.shape[0]
    assert depth == 1, "single-layer step"
    bp, seq, d = x_prompt.shape
    bs, dseq, _ = x_sample.shape
    conv_w = w_dw.shape[2]
    conv_state = w_dw.shape[1] - 1
    hg_w = hg_lower_bounds.shape[1]
    heads = hg_w // HG_HEAD_DIM
    assert conv_w == hg_w and w_in.shape[2] == 2 * conv_w + 4 * hg_w
    assert bs * dseq == ROW_TILE, "the sample rows form one dense-stage row tile"
    alpha = (2.0 * depth) ** 0.25

    xp = x_prompt.reshape(bp * seq, d)
    xs = x_sample.reshape(bs * dseq, d)
    row = lambda a: a.reshape(1, -1)

    proj_s, w_in_bf = _in_proj(xs, w_in[0], row(b_in[0]), tm=ROW_TILE, tn=512, emit_bf16=True)
    (proj_p,) = _in_proj(xp, w_in_bf, row(b_in[0]), tm=2 * ROW_TILE, tn=1024, emit_bf16=False)

    hist_p = jnp.zeros((bp, conv_state, conv_w), F32)
    conv_args = (w_dw[0], row(b_dw[0]), row(conv_norm_g[0]), row(conv_norm_b[0]))
    conv_p, tail_p, (w_out_bf, wg_bf, wu_bf, wd_bf) = _conv_branch(
        proj_p, hist_p, *conv_args, nb_batch=bp, seq=seq, tt=256,
        cast=(w_out[0], w_gate[0], w_up[0], w_down[0]))
    conv_s, tail_s, _ = _conv_branch(proj_s, cache_conv[0], *conv_args, nb_batch=bs, seq=dseq, tt=dseq)

    s0_p = jnp.zeros((bp, heads, HG_HEAD_DIM, HG_HEAD_DIM), F32)
    hg_args = (hg_lower_bounds, row(hg_norm_g[0]))
    hg_s, st_s = _hgrn_branch(proj_s, state_hgrn[0], *hg_args, nb_batch=bs, seq=dseq, tt=8 * dseq, col0=2)
    hg_p, st_p = _hgrn_branch(proj_p, s0_p, *hg_args, nb_batch=bp, seq=seq, tt=256, col0=2)

    out_args = (w_out_bf, row(ln1_g[0]), row(ln1_b[0]), alpha)
    h_s = _out_proj(xs, conv_s, hg_s, *out_args, tm=ROW_TILE)
    h_p = _out_proj(xp, conv_p, hg_p, *out_args, tm=ROW_TILE)

    ffn_args = (wg_bf, wu_bf, wd_bf, row(ln2_g[0]), row(ln2_b[0]), alpha)
    ys = _ffn(h_s, *ffn_args, tm=ROW_TILE, tf=512)
    yp = _ffn(h_p, *ffn_args, tm=ROW_TILE, tf=512)

    return (yp.reshape(bp, seq, d), ys.reshape(bs, dseq, d),
            tail_p[None], st_p[None].astype(x_prompt.dtype),
            tail_s[None].astype(cache_conv.dtype), st_s[None].astype(state_hgrn.dtype))
```

```python
import functools

import jax
import jax.numpy as jnp
from jax import lax
from jax.experimental import pallas as pl
from jax.experimental.pallas import tpu as pltpu

LN_EPS = 1e-5
LOG2E = 1.4426950408889634
HG_HEAD_DIM = 128
HG_BLOCK = 16
SUBLANES = 8
CONV_HALO = 32
VMEM_LIMIT = 56 * 1024 * 1024
ROW_TILE = 512

F32 = jnp.float32
BF16 = jnp.bfloat16


def _sigmoid(x):
    return 0.5 * jnp.tanh(0.5 * x) + 0.5


def _silu(x):
    hx = 0.5 * x
    return hx * jnp.tanh(hx) + hx


def _layer_norm(x, g, b):
    mu = jnp.mean(x, axis=-1, keepdims=True)
    xc = x - mu
    var = jnp.mean(xc * xc, axis=-1, keepdims=True)
    return xc * lax.rsqrt(var + LN_EPS) * g + b


def _params(sem):
    return pltpu.CompilerParams(dimension_semantics=sem, vmem_limit_bytes=VMEM_LIMIT)


def _in_proj_kernel(x_ref, w_ref, b_ref, o_ref, *rest, emit_bf16):
    xb_ref = rest[-1]

    @pl.when(pl.program_id(1) == 0)
    def _():
        xb_ref[...] = x_ref[...].astype(BF16)

    w = w_ref[...].astype(BF16)
    if emit_bf16:
        rest[0][...] = w
    o_ref[...] = jnp.dot(xb_ref[...], w, preferred_element_type=F32) + b_ref[...]


def _in_proj(x, w, b, *, tm, tn, emit_bf16):
    m, d = x.shape
    n = w.shape[1]
    w_spec = pl.BlockSpec((d, tn), lambda i, j: (0, j))
    out_specs = [pl.BlockSpec((tm, tn), lambda i, j: (i, j))]
    out_shape = [jax.ShapeDtypeStruct((m, n), F32)]
    if emit_bf16:
        assert m == tm, "the bf16 weight copy is written once per column tile"
        out_specs.append(w_spec)
        out_shape.append(jax.ShapeDtypeStruct((d, n), BF16))
    return pl.pallas_call(
        functools.partial(_in_proj_kernel, emit_bf16=emit_bf16),
        grid=(m // tm, n // tn),
        in_specs=[pl.BlockSpec((tm, d), lambda i, j: (i, 0)), w_spec,
                  pl.BlockSpec((1, tn), lambda i, j: (0, j))],
        out_specs=out_specs,
        out_shape=out_shape,
        scratch_shapes=[pltpu.VMEM((tm, d), BF16)],
        compiler_params=_params(("arbitrary", "arbitrary")),
        name="in_proj",
    )(x, w, b)


SHIFT_CHUNK = 40


def _conv_kernel(a_ref, gt_ref, hist_ref, w_ref, b_ref, ng_ref, nb_ref, *rest, tt, nt, sps, taps, rows_sub,
                 n_cast):
    cast_in = rest[:n_cast]
    out_ref, tail_ref = rest[n_cast:n_cast + 2]
    cast_out = rest[n_cast + 2:2 * n_cast + 2]
    sh_ref = rest[-1]
    t = pl.program_id(1)
    lead = CONV_HALO - (taps - 1)
    n_ch = w_ref.shape[-1]

    for src, dst in zip(cast_in, cast_out):
        dst[...] = src[...].astype(BF16)

    for n in range(sps):
        rows0 = n * tt
        sh = sh_ref.at[n]

        @pl.when(t == 0)
        def _():
            sh[0, 0:SUBLANES, :] = jnp.zeros((SUBLANES, n_ch), F32)
            sh[0, lead:CONV_HALO, :] = hist_ref[n]

        sh[0, CONV_HALO:CONV_HALO + tt, :] = (a_ref[rows0:rows0 + tt, :]
                                              * _sigmoid(gt_ref[rows0:rows0 + tt, :]))

        n_shift = CONV_HALO + tt - SUBLANES
        for c0 in range(0, n_shift, SHIFT_CHUNK):
            chunk = min(SHIFT_CHUNK, n_shift - c0)
            window = sh[0, c0:c0 + chunk + SUBLANES, :]
            for p in range(1, SUBLANES):
                sh[p, c0:c0 + chunk, :] = pltpu.roll(window, chunk + SUBLANES - p, axis=0)[:chunk]

        for base in range(0, tt, rows_sub):
            acc = None
            for j in range(taps):
                phase = (lead + j) % SUBLANES
                lo = base + (lead + j) - phase
                rows = sh[phase, lo:lo + rows_sub, :].reshape(rows_sub // SUBLANES, SUBLANES, n_ch)
                term = rows * w_ref[j][None]
                acc = term if acc is None else acc + term
            hn = _layer_norm(acc.reshape(rows_sub, n_ch) + b_ref[...], ng_ref[...], nb_ref[...])
            out_ref[rows0 + base:rows0 + base + rows_sub, :] = _silu(hn).astype(out_ref.dtype)

        @pl.when(t == nt - 1)
        def _():
            tail_ref[n] = sh[0, tt + lead:tt + CONV_HALO, :]

        if nt > 1:
            sh[0, 0:CONV_HALO, :] = sh[0, tt:tt + CONV_HALO, :]


def _conv_branch(proj, hist, w_dw, b_dw, ng, nb, *, nb_batch, seq, tt, sps=1, cast=()):
    taps, c = w_dw.shape
    nt = seq // tt
    sps = sps if nt == 1 else 1
    n_groups = nb_batch // sps
    n_steps = n_groups * nt
    rows_sub = min(tt, 32)
    w_rep = jnp.broadcast_to(w_dw[:, None, :], (taps, SUBLANES, c))
    row_idx = lambda b, t: (b * nt + t, 0)
    fixed = lambda b, t: (0, 0)
    seq_blk = pl.BlockSpec((sps, taps - 1, c), lambda b, t: (b, 0, 0))
    cast_specs = [pl.BlockSpec((w.shape[0] // n_steps, w.shape[1]), row_idx) for w in cast]
    assert all(w.shape[0] % (n_steps * 2 * SUBLANES) == 0 for w in cast)
    outs = pl.pallas_call(
        functools.partial(_conv_kernel, tt=tt, nt=nt, sps=sps, taps=taps, rows_sub=rows_sub,
                          n_cast=len(cast)),
        grid=(n_groups, nt),
        in_specs=[
            pl.BlockSpec((sps * tt, c), row_idx),
            pl.BlockSpec((sps * tt, c), lambda b, t: (b * nt + t, 1)),
            seq_blk,
            pl.BlockSpec((taps, SUBLANES, c), lambda b, t: (0, 0, 0)),
            pl.BlockSpec((1, c), fixed),
            pl.BlockSpec((1, c), fixed),
            pl.BlockSpec((1, c), fixed),
        ] + cast_specs,
        out_specs=[pl.BlockSpec((sps * tt, c), row_idx), seq_blk] + cast_specs,
        out_shape=[
            jax.ShapeDtypeStruct((nb_batch * seq, c), BF16),
            jax.ShapeDtypeStruct((nb_batch, taps - 1, c), F32),
        ] + [jax.ShapeDtypeStruct(w.shape, BF16) for w in cast],
        scratch_shapes=[pltpu.VMEM((sps, SUBLANES, CONV_HALO + tt, c), F32)],
        compiler_params=_params(("arbitrary", "arbitrary")),
        name="conv_branch",
    )(proj, proj, hist, w_rep, b_dw, ng, nb, *cast)
    return outs[0], outs[1], outs[2:]


def _split3(x):
    hi = x.astype(BF16)
    r1 = x - hi.astype(F32)
    mid = r1.astype(BF16)
    lo = (r1 - mid.astype(F32)).astype(BF16)
    return hi, mid, lo


def _hgrn_kernel(q_ref, f_ref, v_ref, g_ref, s0_ref, lbp_ref, ng_ref, out_ref, sout_ref,
                 st_ref, lc_s, lck_s, q_s, v_s, o_s, *, tt, nt, heads, chunk, n_streams):
    t = pl.program_id(1)
    hd = HG_HEAD_DIM
    half = HG_BLOCK // 2

    @pl.when(t == 0)
    def _():
        for n in range(n_streams):
            for h in range(heads):
                st_ref[n * heads + h] = s0_ref[n, h].T

    lbp = lbp_ref[...]
    e = jnp.exp(lbp - jnp.max(lbp, axis=0, keepdims=True))
    lb = e[0:1, :] / jnp.sum(e, axis=0, keepdims=True)

    def to_heads(dst, rows, val):
        for h in range(heads):
            dst[h, rows, :] = val[:, h * hd:(h + 1) * hd]

    k = (0.5 * (1.0 - lb)) * (1.0 - jnp.tanh(0.5 * f_ref[...]))
    log2f = jnp.log(1.0 - k) * LOG2E
    log2k = jnp.log(k) * LOG2E
    to_heads(q_s, slice(0, tt), _silu(q_ref[...]))
    to_heads(v_s, slice(0, tt), v_ref[...])

    ri = lax.broadcasted_iota(jnp.int32, (chunk, chunk), 0)
    ci = lax.broadcasted_iota(jnp.int32, (chunk, chunk), 1)
    same_block_causal = (ci <= ri) & ((ri - ci) <= (ri & (HG_BLOCK - 1)))
    tri = jnp.where(same_block_causal, 1.0, 0.0).astype(BF16)
    for c in range(tt // chunk):
        rows = slice(c * chunk, (c + 1) * chunk)
        hi, mid, lo = _split3(log2f[rows, :])
        lc = (jnp.dot(tri, hi, preferred_element_type=F32)
              + jnp.dot(tri, mid, preferred_element_type=F32)
              + jnp.dot(tri, lo, preferred_element_type=F32))
        to_heads(lc_s, rows, lc)
        to_heads(lck_s, rows, lc - log2k[rows, :])

    ones = jnp.ones((hd, hd), BF16)
    row_id = lax.broadcasted_iota(jnp.int32, (half, hd), 0)

    def block_body(blk, carry):
        r0 = pl.multiple_of(blk * HG_BLOCK, HG_BLOCK)
        st_base = blk * heads if n_streams > 1 else 0
        for h in range(heads):
            cs = slice(h * hd, (h + 1) * hd)
            lc = lc_s[h, pl.ds(r0, HG_BLOCK), :]
            lck = lck_s[h, pl.ds(r0, HG_BLOCK), :]
            q = q_s[h, pl.ds(r0, HG_BLOCK), :]
            v = v_s[h, pl.ds(r0, HG_BLOCK), :]
            lc_last = lc[HG_BLOCK - 1:HG_BLOCK, :]
            q_dec = q * jnp.exp2(lc)
            k_end = jnp.exp2(lc_last - lck)
            g_blk = jnp.exp2(lc_last)
            st = st_ref[st_base + h]
            o = lax.dot_general(q_dec.astype(BF16), st.astype(BF16), (((1,), (1,)), ((), ())),
                                preferred_element_type=F32)
            q_top, q_bot = q[:half], q[half:]
            lc_top, lc_bot = lc[:half], lc[half:]
            parts = []
            for s in range(HG_BLOCK):
                c = lck_s[h, pl.ds(r0 + s, 1), :]
                if s < half:
                    top = q_top * jnp.exp2(lc_top - c)
                    if s > 0:
                        top = jnp.where(row_id >= s, top, 0.0)
                    parts.append(top)
                    parts.append(q_bot * jnp.exp2(lc_bot - c))
                else:
                    bot = q_bot * jnp.exp2(lc_bot - c)
                    if s > half:
                        bot = jnp.where(row_id >= s - half, bot, 0.0)
                    parts.append(bot)
            sc = jnp.dot(jnp.concatenate(parts, axis=0).astype(BF16), ones, preferred_element_type=F32)
            o_top = None
            o_bot = None
            for s in range(HG_BLOCK):
                vs = v_s[h, pl.ds(r0 + s, 1), :]
                if s < half:
                    top = sc[2 * half * s:2 * half * s + half] * vs
                    bot = sc[2 * half * s + half:2 * half * (s + 1)] * vs
                    o_top = top if o_top is None else o_top + top
                else:
                    lo = 2 * half * half + half * (s - half)
                    bot = sc[lo:lo + half] * vs
                o_bot = bot if o_bot is None else o_bot + bot
            o_s[pl.ds(r0, HG_BLOCK), cs] = o + jnp.concatenate([o_top, o_bot], axis=0)
            d_st = lax.dot_general(v.astype(BF16), k_end.astype(BF16), (((0,), (0,)), ((), ())),
                                   preferred_element_type=F32)
            st_ref[st_base + h] = st * g_blk + d_st
        return carry

    lax.fori_loop(0, tt // HG_BLOCK, block_body, 0, unroll=min(4, tt // HG_BLOCK))

    for h in range(heads):
        cs = slice(h * hd, (h + 1) * hd)
        o = o_s[:, cs]
        ms = jnp.mean(o * o, axis=-1, keepdims=True)
        on = o * lax.rsqrt(ms + LN_EPS) * ng_ref[...]
        out_ref[:, cs] = (on * _silu(g_ref[:, cs])).astype(out_ref.dtype)

    @pl.when(t == nt - 1)
    def _():
        for n in range(n_streams):
            for h in range(heads):
                sout_ref[n, h] = st_ref[n * heads + h].T


def _hgrn_branch(proj, s0, lbp, ng, *, nb_batch, seq, tt, col0):
    w = lbp.shape[1]
    heads = w // HG_HEAD_DIM
    if seq == HG_BLOCK:
        n_streams = tt // seq
        n_groups, nt = nb_batch // n_streams, 1
    else:
        n_streams = 1
        n_groups, nt = nb_batch, seq // tt
    chunk = min(tt, 128)
    col = lambda k: (lambda g, t: (g * nt + t, col0 + k))
    state_spec = pl.BlockSpec((n_streams, heads, HG_HEAD_DIM, HG_HEAD_DIM), lambda g, t: (g, 0, 0, 0))
    slab = pltpu.VMEM((heads, tt, HG_HEAD_DIM), F32)
    return pl.pallas_call(
        functools.partial(_hgrn_kernel, tt=tt, nt=nt, heads=heads, chunk=chunk, n_streams=n_streams),
        grid=(n_groups, nt),
        in_specs=[
            pl.BlockSpec((tt, w), col(0)),
            pl.BlockSpec((tt, w), col(1)),
            pl.BlockSpec((tt, w), col(2)),
            pl.BlockSpec((tt, w), col(3)),
            state_spec,
            pl.BlockSpec(lbp.shape, lambda g, t: (0, 0)),
            pl.BlockSpec((1, HG_HEAD_DIM), lambda g, t: (0, 0)),
        ],
        out_specs=[
            pl.BlockSpec((tt, w), lambda g, t: (g * nt + t, 0)),
            state_spec,
        ],
        out_shape=[
            jax.ShapeDtypeStruct((nb_batch * seq, w), BF16),
            jax.ShapeDtypeStruct((nb_batch, heads, HG_HEAD_DIM, HG_HEAD_DIM), F32),
        ],
        scratch_shapes=[
            pltpu.VMEM((n_streams * heads, HG_HEAD_DIM, HG_HEAD_DIM), F32),
            slab, slab, slab, slab,
            pltpu.VMEM((tt, w), F32),
        ],
        compiler_params=_params(("arbitrary", "arbitrary")),
        name="hgrn_branch",
    )(proj, proj, proj, proj, s0, lbp, ng)


def _out_proj_kernel(x_ref, c_ref, h_ref, wc_ref, wh_ref, g_ref, b_ref, o_ref, *, alpha, rows_sub):
    for r0 in range(0, x_ref.shape[0], rows_sub):
        rows = slice(r0, r0 + rows_sub)
        mixed = (jnp.dot(c_ref[rows, :], wc_ref[...], preferred_element_type=F32)
                 + jnp.dot(h_ref[rows, :], wh_ref[...], preferred_element_type=F32))
        o_ref[rows, :] = _layer_norm(alpha * x_ref[rows, :] + mixed, g_ref[...], b_ref[...])


def _out_proj(x, conv, hg, w_bf, g, b, alpha, *, tm, rows_sub=128):
    m, d = x.shape
    c = conv.shape[1]
    row = lambda i: (i, 0)
    fixed = lambda i: (0, 0)
    return pl.pallas_call(
        functools.partial(_out_proj_kernel, alpha=alpha, rows_sub=rows_sub),
        grid=(m // tm,),
        in_specs=[
            pl.BlockSpec((tm, d), row),
            pl.BlockSpec((tm, c), row),
            pl.BlockSpec((tm, c), row),
            pl.BlockSpec((c, d), fixed),
            pl.BlockSpec((c, d), lambda i: (1, 0)),
            pl.BlockSpec((1, d), fixed),
            pl.BlockSpec((1, d), fixed),
        ],
        out_specs=pl.BlockSpec((tm, d), row),
        out_shape=jax.ShapeDtypeStruct((m, d), F32),
        compiler_params=_params(("arbitrary",)),
        name="out_proj",
    )(x, conv, hg, w_bf, w_bf, g, b)


def _ffn_kernel(h_ref, wg_ref, wu_ref, wd_ref, g_ref, b_ref, y_ref, hb_ref, *, n_ff_tiles, alpha):
    j = pl.program_id(1)

    @pl.when(j == 0)
    def _():
        hb_ref[...] = h_ref[...].astype(BF16)
        y_ref[...] = jnp.zeros_like(y_ref)

    hb = hb_ref[...]
    gate = jnp.dot(hb, wg_ref[...], preferred_element_type=F32)
    up = jnp.dot(hb, wu_ref[...], preferred_element_type=F32)
    act = (_silu(gate) * up).astype(BF16)
    y_ref[...] += jnp.dot(act, wd_ref[...], preferred_element_type=F32)

    @pl.when(j == n_ff_tiles - 1)
    def _():
        y_ref[...] = _layer_norm(alpha * h_ref[...] + y_ref[...], g_ref[...], b_ref[...])


def _ffn(h, wg_bf, wu_bf, wd_bf, g, b, alpha, *, tm, tf):
    m, d = h.shape
    dff = wg_bf.shape[1]
    nj = dff // tf
    up_spec = pl.BlockSpec((d, tf), lambda i, j: (0, j))
    fixed = lambda i, j: (0, 0)
    return pl.pallas_call(
        functools.partial(_ffn_kernel, n_ff_tiles=nj, alpha=alpha),
        grid=(m // tm, nj),
        in_specs=[
            pl.BlockSpec((tm, d), lambda i, j: (i, 0)),
            up_spec, up_spec,
            pl.BlockSpec((tf, d), lambda i, j: (j, 0)),
            pl.BlockSpec((1, d), fixed),
            pl.BlockSpec((1, d), fixed),
        ],
        out_specs=pl.BlockSpec((tm, d), lambda i, j: (i, 0)),
        out_shape=jax.ShapeDtypeStruct((m, d), F32),
        scratch_shapes=[pltpu.VMEM((tm, d), BF16)],
        compiler_params=_params(("arbitrary", "arbitrary")),
        name="ffn",
    )(h, wg_bf, wu_bf, wd_bf, g, b)


def kernel(x_prompt, x_sample, cache_conv, state_hgrn, w_in, b_in, w_dw, b_dw, conv_norm_g, conv_norm_b,
           hg_lower_bounds, hg_norm_g, w_out, ln1_g, ln1_b, w_gate, w_up, w_down, ln2_g, ln2_b):
    depth = w_in.shape[0]
    assert depth == 1, "single-layer step"
    bp, seq, d = x_prompt.shape
    bs, dseq, _ = x_sample.shape
    conv_w = w_dw.shape[2]
    conv_state = w_dw.shape[1] - 1
    hg_w = hg_lower_bounds.shape[1]
    heads = hg_w // HG_HEAD_DIM
    assert conv_w == hg_w and w_in.shape[2] == 2 * conv_w + 4 * hg_w
    assert bs * dseq == ROW_TILE, "the sample rows form one dense-stage row tile"
    alpha = (2.0 * depth) ** 0.25

    xp = x_prompt.reshape(bp * seq, d)
    xs = x_sample.reshape(bs * dseq, d)
    row = lambda a: a.reshape(1, -1)

    proj_s, w_in_bf = _in_proj(xs, w_in[0], row(b_in[0]), tm=ROW_TILE, tn=512, emit_bf16=True)
    (proj_p,) = _in_proj(xp, w_in_bf, row(b_in[0]), tm=2 * ROW_TILE, tn=1024, emit_bf16=False)

    hist_p = jnp.zeros((bp, conv_state, conv_w), F32)
    conv_args = (w_dw[0], row(b_dw[0]), row(conv_norm_g[0]), row(conv_norm_b[0]))
    conv_p, tail_p, (w_out_bf, wg_bf, wu_bf, wd_bf) = _conv_branch(
        proj_p, hist_p, *conv_args, nb_batch=bp, seq=seq, tt=256,
        cast=(w_out[0], w_gate[0], w_up[0], w_down[0]))
    conv_s, tail_s, _ = _conv_branch(proj_s, cache_conv[0], *conv_args, nb_batch=bs, seq=dseq, tt=dseq,
                                     sps=4)

    s0_p = jnp.zeros((bp, heads, HG_HEAD_DIM, HG_HEAD_DIM), F32)
    hg_args = (hg_lower_bounds, row(hg_norm_g[0]))
    hg_s, st_s = _hgrn_branch(proj_s, state_hgrn[0], *hg_args, nb_batch=bs, seq=dseq, tt=8 * dseq, col0=2)
    hg_p, st_p = _hgrn_branch(proj_p, s0_p, *hg_args, nb_batch=bp, seq=seq, tt=512, col0=2)

    out_args = (w_out_bf, row(ln1_g[0]), row(ln1_b[0]), alpha)
    h_s = _out_proj(xs, conv_s, hg_s, *out_args, tm=ROW_TILE)
    h_p = _out_proj(xp, conv_p, hg_p, *out_args, tm=ROW_TILE)

    ffn_args = (wg_bf, wu_bf, wd_bf, row(ln2_g[0]), row(ln2_b[0]), alpha)
    ys = _ffn(h_s, *ffn_args, tm=ROW_TILE, tf=512)
    yp = _ffn(h_p, *ffn_args, tm=ROW_TILE, tf=512)

    return (yp.reshape(bp, seq, d), ys.reshape(bs, dseq, d),
            tail_p[None], st_p[None].astype(x_prompt.dtype),
            tail_s[None].astype(cache_conv.dtype), st_s[None].astype(state_hgrn.dtype))
```

```python
import functools

import jax
import jax.numpy as jnp
from jax import lax
from jax.experimental import pallas as pl
from jax.experimental.pallas import tpu as pltpu

LN_EPS = 1e-5
LOG2E = 1.4426950408889634
HG_HEAD_DIM = 128
HG_BLOCK = 16
SUBLANES = 8
CONV_HALO = 32
VMEM_LIMIT = 56 * 1024 * 1024

ROW_TILE = 512
IN_PROJ_PROMPT_ROWS = 1024
IN_PROJ_PROMPT_COLS = 1024
IN_PROJ_CAST_COLS = 512
CONV_PROMPT_ROWS = 256
CONV_SAMPLE_STREAMS = 4
HGRN_PROMPT_ROWS = 512
HGRN_SAMPLE_STREAMS = 8
OUT_PROJ_SUB_ROWS = 128
FFN_TILE = 512

F32 = jnp.float32
BF16 = jnp.bfloat16


def _sigmoid(x):
    return 0.5 * jnp.tanh(0.5 * x) + 0.5


def _silu(x):
    hx = 0.5 * x
    return hx * jnp.tanh(hx) + hx


def _layer_norm(x, g, b):
    mu = jnp.mean(x, axis=-1, keepdims=True)
    xc = x - mu
    var = jnp.mean(xc * xc, axis=-1, keepdims=True)
    return xc * lax.rsqrt(var + LN_EPS) * g + b


def _params(sem):
    return pltpu.CompilerParams(dimension_semantics=sem, vmem_limit_bytes=VMEM_LIMIT)


def _in_proj_kernel(x_ref, w_ref, b_ref, o_ref, *rest, emit_bf16):
    xb_ref = rest[-1]

    @pl.when(pl.program_id(1) == 0)
    def _():
        xb_ref[...] = x_ref[...].astype(BF16)

    w = w_ref[...].astype(BF16)
    if emit_bf16:
        rest[0][...] = w
    o_ref[...] = jnp.dot(xb_ref[...], w, preferred_element_type=F32) + b_ref[...]


def _in_proj(x, w, b, *, tm, tn, emit_bf16):
    m, d = x.shape
    n = w.shape[1]
    w_spec = pl.BlockSpec((d, tn), lambda i, j: (0, j))
    out_specs = [pl.BlockSpec((tm, tn), lambda i, j: (i, j))]
    out_shape = [jax.ShapeDtypeStruct((m, n), F32)]
    if emit_bf16:
        assert m == tm, "the bf16 weight copy is written once per column tile"
        out_specs.append(w_spec)
        out_shape.append(jax.ShapeDtypeStruct((d, n), BF16))
    return pl.pallas_call(
        functools.partial(_in_proj_kernel, emit_bf16=emit_bf16),
        grid=(m // tm, n // tn),
        in_specs=[pl.BlockSpec((tm, d), lambda i, j: (i, 0)), w_spec,
                  pl.BlockSpec((1, tn), lambda i, j: (0, j))],
        out_specs=out_specs,
        out_shape=out_shape,
        scratch_shapes=[pltpu.VMEM((tm, d), BF16)],
        compiler_params=_params(("arbitrary", "arbitrary")),
        name="in_proj",
    )(x, w, b)


SHIFT_CHUNK = 40


def _conv_kernel(a_ref, gt_ref, hist_ref, w_ref, b_ref, ng_ref, nb_ref, *rest, tt, nt, sps, taps, rows_sub,
                 n_cast):
    cast_in = rest[:n_cast]
    out_ref, tail_ref = rest[n_cast:n_cast + 2]
    cast_out = rest[n_cast + 2:2 * n_cast + 2]
    sh_ref = rest[-1]
    t = pl.program_id(1)
    lead = CONV_HALO - (taps - 1)
    n_ch = w_ref.shape[-1]

    for src, dst in zip(cast_in, cast_out):
        dst[...] = src[...].astype(BF16)

    for n in range(sps):
        rows0 = n * tt
        sh = sh_ref.at[n]

        @pl.when(t == 0)
        def _():
            sh[0, 0:SUBLANES, :] = jnp.zeros((SUBLANES, n_ch), F32)
            sh[0, lead:CONV_HALO, :] = hist_ref[n]

        sh[0, CONV_HALO:CONV_HALO + tt, :] = (a_ref[rows0:rows0 + tt, :]
                                              * _sigmoid(gt_ref[rows0:rows0 + tt, :]))

        n_shift = CONV_HALO + tt - SUBLANES
        for c0 in range(0, n_shift, SHIFT_CHUNK):
            chunk = min(SHIFT_CHUNK, n_shift - c0)
            window = sh[0, c0:c0 + chunk + SUBLANES, :]
            for p in range(1, SUBLANES):
                sh[p, c0:c0 + chunk, :] = pltpu.roll(window, chunk + SUBLANES - p, axis=0)[:chunk]

        for base in range(0, tt, rows_sub):
            acc = None
            for j in range(taps):
                phase = (lead + j) % SUBLANES
                lo = base + (lead + j) - phase
                rows = sh[phase, lo:lo + rows_sub, :].reshape(rows_sub // SUBLANES, SUBLANES, n_ch)
                term = rows * w_ref[j][None]
                acc = term if acc is None else acc + term
            hn = _layer_norm(acc.reshape(rows_sub, n_ch) + b_ref[...], ng_ref[...], nb_ref[...])
            out_ref[rows0 + base:rows0 + base + rows_sub, :] = _silu(hn).astype(out_ref.dtype)

        @pl.when(t == nt - 1)
        def _():
            tail_ref[n] = sh[0, tt + lead:tt + CONV_HALO, :]

        if nt > 1:
            sh[0, 0:CONV_HALO, :] = sh[0, tt:tt + CONV_HALO, :]


def _conv_branch(proj, hist, w_dw, b_dw, ng, nb, *, nb_batch, seq, tt, sps=1, cast=()):
    taps, c = w_dw.shape
    nt = seq // tt
    sps = sps if nt == 1 else 1
    n_groups = nb_batch // sps
    n_steps = n_groups * nt
    rows_sub = min(tt, 32)
    w_rep = jnp.broadcast_to(w_dw[:, None, :], (taps, SUBLANES, c))
    row_idx = lambda b, t: (b * nt + t, 0)
    fixed = lambda b, t: (0, 0)
    seq_blk = pl.BlockSpec((sps, taps - 1, c), lambda b, t: (b, 0, 0))
    cast_specs = [pl.BlockSpec((w.shape[0] // n_steps, w.shape[1]), row_idx) for w in cast]
    assert all(w.shape[0] % (n_steps * 2 * SUBLANES) == 0 for w in cast)
    outs = pl.pallas_call(
        functools.partial(_conv_kernel, tt=tt, nt=nt, sps=sps, taps=taps, rows_sub=rows_sub,
                          n_cast=len(cast)),
        grid=(n_groups, nt),
        in_specs=[
            pl.BlockSpec((sps * tt, c), row_idx),
            pl.BlockSpec((sps * tt, c), lambda b, t: (b * nt + t, 1)),
            seq_blk,
            pl.BlockSpec((taps, SUBLANES, c), lambda b, t: (0, 0, 0)),
            pl.BlockSpec((1, c), fixed),
            pl.BlockSpec((1, c), fixed),
            pl.BlockSpec((1, c), fixed),
        ] + cast_specs,
        out_specs=[pl.BlockSpec((sps * tt, c), row_idx), seq_blk] + cast_specs,
        out_shape=[
            jax.ShapeDtypeStruct((nb_batch * seq, c), BF16),
            jax.ShapeDtypeStruct((nb_batch, taps - 1, c), F32),
        ] + [jax.ShapeDtypeStruct(w.shape, BF16) for w in cast],
        scratch_shapes=[pltpu.VMEM((sps, SUBLANES, CONV_HALO + tt, c), F32)],
        compiler_params=_params(("arbitrary", "arbitrary")),
        name="conv_branch",
    )(proj, proj, hist, w_rep, b_dw, ng, nb, *cast)
    return outs[0], outs[1], outs[2:]


def _split3(x):
    hi = x.astype(BF16)
    r1 = x - hi.astype(F32)
    mid = r1.astype(BF16)
    lo = (r1 - mid.astype(F32)).astype(BF16)
    return hi, mid, lo


def _hgrn_kernel(q_ref, f_ref, v_ref, g_ref, s0_ref, lbp_ref, ng_ref, out_ref, sout_ref,
                 st_ref, lc_s, lck_s, q_s, v_s, o_s, *, tt, nt, heads, chunk, n_streams):
    t = pl.program_id(1)
    hd = HG_HEAD_DIM
    half = HG_BLOCK // 2

    @pl.when(t == 0)
    def _():
        for n in range(n_streams):
            for h in range(heads):
                st_ref[n * heads + h] = s0_ref[n, h].T

    lbp = lbp_ref[...]
    e = jnp.exp(lbp - jnp.max(lbp, axis=0, keepdims=True))
    lb = e[0:1, :] / jnp.sum(e, axis=0, keepdims=True)

    def to_heads(dst, rows, val):
        for h in range(heads):
            dst[h, rows, :] = val[:, h * hd:(h + 1) * hd]

    k = (0.5 * (1.0 - lb)) * (1.0 - jnp.tanh(0.5 * f_ref[...]))
    log2f = jnp.log(1.0 - k) * LOG2E
    log2k = jnp.log(k) * LOG2E
    to_heads(q_s, slice(0, tt), _silu(q_ref[...]))
    to_heads(v_s, slice(0, tt), v_ref[...])

    ri = lax.broadcasted_iota(jnp.int32, (chunk, chunk), 0)
    ci = lax.broadcasted_iota(jnp.int32, (chunk, chunk), 1)
    same_block_causal = (ci <= ri) & ((ri - ci) <= (ri & (HG_BLOCK - 1)))
    tri = jnp.where(same_block_causal, 1.0, 0.0).astype(BF16)
    for c in range(tt // chunk):
        rows = slice(c * chunk, (c + 1) * chunk)
        hi, mid, lo = _split3(log2f[rows, :])
        lc = (jnp.dot(tri, hi, preferred_element_type=F32)
              + jnp.dot(tri, mid, preferred_element_type=F32)
              + jnp.dot(tri, lo, preferred_element_type=F32))
        to_heads(lc_s, rows, lc)
        to_heads(lck_s, rows, lc - log2k[rows, :])

    ones = jnp.ones((hd, hd), BF16)
    row_id = lax.broadcasted_iota(jnp.int32, (half, hd), 0)

    def block_body(blk, carry):
        r0 = pl.multiple_of(blk * HG_BLOCK, HG_BLOCK)
        st_base = blk * heads if n_streams > 1 else 0
        for h in range(heads):
            cs = slice(h * hd, (h + 1) * hd)
            lc = lc_s[h, pl.ds(r0, HG_BLOCK), :]
            lck = lck_s[h, pl.ds(r0, HG_BLOCK), :]
            q = q_s[h, pl.ds(r0, HG_BLOCK), :]
            v = v_s[h, pl.ds(r0, HG_BLOCK), :]
            lc_last = lc[HG_BLOCK - 1:HG_BLOCK, :]
            q_dec = q * jnp.exp2(lc)
            k_end = jnp.exp2(lc_last - lck)
            g_blk = jnp.exp2(lc_last)
            st = st_ref[st_base + h]
            o = lax.dot_general(q_dec.astype(BF16), st.astype(BF16), (((1,), (1,)), ((), ())),
                                preferred_element_type=F32)
            q_top, q_bot = q[:half], q[half:]
            lc_top, lc_bot = lc[:half], lc[half:]
            o_top = None
            bots = []
            for s in range(HG_BLOCK):
                c = lck_s[h, pl.ds(r0 + s, 1), :]
                if s < half:
                    top = q_top * jnp.exp2(lc_top - c)
                    if s > 0:
                        top = jnp.where(row_id >= s, top, 0.0)
                    top = jnp.sum(top, axis=-1, keepdims=True) * v_s[h, pl.ds(r0 + s, 1), :]
                    o_top = top if o_top is None else o_top + top
                bot = q_bot * jnp.exp2(lc_bot - c)
                if s > half:
                    bot = jnp.where(row_id >= s - half, bot, 0.0)
                bots.append(bot)
            sc = jnp.dot(jnp.concatenate(bots, axis=0).astype(BF16), ones, preferred_element_type=F32)
            o_bot = None
            for s in range(HG_BLOCK):
                bot = sc[half * s:half * (s + 1)] * v_s[h, pl.ds(r0 + s, 1), :]
                o_bot = bot if o_bot is None else o_bot + bot
            o_s[pl.ds(r0, HG_BLOCK), cs] = o + jnp.concatenate([o_top, o_bot], axis=0)
            d_st = lax.dot_general(v.astype(BF16), k_end.astype(BF16), (((0,), (0,)), ((), ())),
                                   preferred_element_type=F32)
            st_ref[st_base + h] = st * g_blk + d_st
        return carry

    lax.fori_loop(0, tt // HG_BLOCK, block_body, 0, unroll=min(4, tt // HG_BLOCK))

    for h in range(heads):
        cs = slice(h * hd, (h + 1) * hd)
        o = o_s[:, cs]
        ms = jnp.mean(o * o, axis=-1, keepdims=True)
        on = o * lax.rsqrt(ms + LN_EPS) * ng_ref[...]
        out_ref[:, cs] = (on * _silu(g_ref[:, cs])).astype(out_ref.dtype)

    @pl.when(t == nt - 1)
    def _():
        for n in range(n_streams):
            for h in range(heads):
                sout_ref[n, h] = st_ref[n * heads + h].T


def _hgrn_branch(proj, s0, lbp, ng, *, nb_batch, seq, tt, col0):
    w = lbp.shape[1]
    heads = w // HG_HEAD_DIM
    if seq == HG_BLOCK:
        n_streams = tt // seq
        n_groups, nt = nb_batch // n_streams, 1
    else:
        n_streams = 1
        n_groups, nt = nb_batch, seq // tt
    chunk = min(tt, 128)
    col = lambda k: (lambda g, t: (g * nt + t, col0 + k))
    state_spec = pl.BlockSpec((n_streams, heads, HG_HEAD_DIM, HG_HEAD_DIM), lambda g, t: (g, 0, 0, 0))
    slab = pltpu.VMEM((heads, tt, HG_HEAD_DIM), F32)
    return pl.pallas_call(
        functools.partial(_hgrn_kernel, tt=tt, nt=nt, heads=heads, chunk=chunk, n_streams=n_streams),
        grid=(n_groups, nt),
        in_specs=[
            pl.BlockSpec((tt, w), col(0)),
            pl.BlockSpec((tt, w), col(1)),
            pl.BlockSpec((tt, w), col(2)),
            pl.BlockSpec((tt, w), col(3)),
            state_spec,
            pl.BlockSpec(lbp.shape, lambda g, t: (0, 0)),
            pl.BlockSpec((1, HG_HEAD_DIM), lambda g, t: (0, 0)),
        ],
        out_specs=[
            pl.BlockSpec((tt, w), lambda g, t: (g * nt + t, 0)),
            state_spec,
        ],
        out_shape=[
            jax.ShapeDtypeStruct((nb_batch * seq, w), BF16),
            jax.ShapeDtypeStruct((nb_batch, heads, HG_HEAD_DIM, HG_HEAD_DIM), F32),
        ],
        scratch_shapes=[
            pltpu.VMEM((n_streams * heads, HG_HEAD_DIM, HG_HEAD_DIM), F32),
            slab, slab, slab, slab,
            pltpu.VMEM((tt, w), F32),
        ],
        compiler_params=_params(("arbitrary", "arbitrary")),
        name="hgrn_branch",
    )(proj, proj, proj, proj, s0, lbp, ng)


def _out_proj_kernel(x_ref, c_ref, h_ref, wc_ref, wh_ref, g_ref, b_ref, o_ref, *, alpha, rows_sub):
    for r0 in range(0, x_ref.shape[0], rows_sub):
        rows = slice(r0, r0 + rows_sub)
        mixed = (jnp.dot(c_ref[rows, :], wc_ref[...], preferred_element_type=F32)
                 + jnp.dot(h_ref[rows, :], wh_ref[...], preferred_element_type=F32))
        o_ref[rows, :] = _layer_norm(alpha * x_ref[rows, :] + mixed, g_ref[...], b_ref[...])


def _out_proj(x, conv, hg, w_bf, g, b, alpha, *, tm, rows_sub=OUT_PROJ_SUB_ROWS):
    m, d = x.shape
    c = conv.shape[1]
    row = lambda i: (i, 0)
    fixed = lambda i: (0, 0)
    return pl.pallas_call(
        functools.partial(_out_proj_kernel, alpha=alpha, rows_sub=rows_sub),
        grid=(m // tm,),
        in_specs=[
            pl.BlockSpec((tm, d), row),
            pl.BlockSpec((tm, c), row),
            pl.BlockSpec((tm, c), row),
            pl.BlockSpec((c, d), fixed),
            pl.BlockSpec((c, d), lambda i: (1, 0)),
            pl.BlockSpec((1, d), fixed),
            pl.BlockSpec((1, d), fixed),
        ],
        out_specs=pl.BlockSpec((tm, d), row),
        out_shape=jax.ShapeDtypeStruct((m, d), F32),
        compiler_params=_params(("arbitrary",)),
        name="out_proj",
    )(x, conv, hg, w_bf, w_bf, g, b)


def _ffn_kernel(h_ref, wg_ref, wu_ref, wd_ref, g_ref, b_ref, y_ref, hb_ref, *, n_ff_tiles, alpha):
    j = pl.program_id(1)

    @pl.when(j == 0)
    def _():
        hb_ref[...] = h_ref[...].astype(BF16)
        y_ref[...] = jnp.zeros_like(y_ref)

    hb = hb_ref[...]
    gate = jnp.dot(hb, wg_ref[...], preferred_element_type=F32)
    up = jnp.dot(hb, wu_ref[...], preferred_element_type=F32)
    act = (_silu(gate) * up).astype(BF16)
    y_ref[...] += jnp.dot(act, wd_ref[...], preferred_element_type=F32)

    @pl.when(j == n_ff_tiles - 1)
    def _():
        y_ref[...] = _layer_norm(alpha * h_ref[...] + y_ref[...], g_ref[...], b_ref[...])


def _ffn(h, wg_bf, wu_bf, wd_bf, g, b, alpha, *, tm, tf):
    m, d = h.shape
    dff = wg_bf.shape[1]
    nj = dff // tf
    up_spec = pl.BlockSpec((d, tf), lambda i, j: (0, j))
    fixed = lambda i, j: (0, 0)
    return pl.pallas_call(
        functools.partial(_ffn_kernel, n_ff_tiles=nj, alpha=alpha),
        grid=(m // tm, nj),
        in_specs=[
            pl.BlockSpec((tm, d), lambda i, j: (i, 0)),
            up_spec, up_spec,
            pl.BlockSpec((tf, d), lambda i, j: (j, 0)),
            pl.BlockSpec((1, d), fixed),
            pl.BlockSpec((1, d), fixed),
        ],
        out_specs=pl.BlockSpec((tm, d), lambda i, j: (i, 0)),
        out_shape=jax.ShapeDtypeStruct((m, d), F32),
        scratch_shapes=[pltpu.VMEM((tm, d), BF16)],
        compiler_params=_params(("arbitrary", "arbitrary")),
        name="ffn",
    )(h, wg_bf, wu_bf, wd_bf, g, b)


def kernel(x_prompt, x_sample, cache_conv, state_hgrn, w_in, b_in, w_dw, b_dw, conv_norm_g, conv_norm_b,
           hg_lower_bounds, hg_norm_g, w_out, ln1_g, ln1_b, w_gate, w_up, w_down, ln2_g, ln2_b):
    depth = w_in.shape[0]
    assert depth == 1, "single-layer step"
    bp, seq, d = x_prompt.shape
    bs, dseq, _ = x_sample.shape
    conv_w = w_dw.shape[2]
    conv_state = w_dw.shape[1] - 1
    hg_w = hg_lower_bounds.shape[1]
    heads = hg_w // HG_HEAD_DIM
    assert conv_w == hg_w and w_in.shape[2] == 2 * conv_w + 4 * hg_w
    assert bs * dseq == ROW_TILE, "the sample rows form one dense-stage row tile"
    alpha = (2.0 * depth) ** 0.25

    xp = x_prompt.reshape(bp * seq, d)
    xs = x_sample.reshape(bs * dseq, d)
    row = lambda a: a.reshape(1, -1)

    proj_s, w_in_bf = _in_proj(xs, w_in[0], row(b_in[0]), tm=ROW_TILE, tn=IN_PROJ_CAST_COLS, emit_bf16=True)
    (proj_p,) = _in_proj(xp, w_in_bf, row(b_in[0]), tm=IN_PROJ_PROMPT_ROWS, tn=IN_PROJ_PROMPT_COLS,
                         emit_bf16=False)

    hist_p = jnp.zeros((bp, conv_state, conv_w), F32)
    conv_args = (w_dw[0], row(b_dw[0]), row(conv_norm_g[0]), row(conv_norm_b[0]))
    conv_p, tail_p, (w_out_bf, wg_bf, wu_bf, wd_bf) = _conv_branch(
        proj_p, hist_p, *conv_args, nb_batch=bp, seq=seq, tt=CONV_PROMPT_ROWS,
        cast=(w_out[0], w_gate[0], w_up[0], w_down[0]))
    conv_s, tail_s, _ = _conv_branch(proj_s, cache_conv[0], *conv_args, nb_batch=bs, seq=dseq, tt=dseq,
                                     sps=CONV_SAMPLE_STREAMS)

    s0_p = jnp.zeros((bp, heads, HG_HEAD_DIM, HG_HEAD_DIM), F32)
    hg_args = (hg_lower_bounds, row(hg_norm_g[0]))
    hg_s, st_s = _hgrn_branch(proj_s, state_hgrn[0], *hg_args, nb_batch=bs, seq=dseq,
                              tt=HGRN_SAMPLE_STREAMS * dseq, col0=2)
    hg_p, st_p = _hgrn_branch(proj_p, s0_p, *hg_args, nb_batch=bp, seq=seq, tt=HGRN_PROMPT_ROWS, col0=2)

    out_args = (w_out_bf, row(ln1_g[0]), row(ln1_b[0]), alpha)
    h_s = _out_proj(xs, conv_s, hg_s, *out_args, tm=ROW_TILE)
    h_p = _out_proj(xp, conv_p, hg_p, *out_args, tm=ROW_TILE)

    ffn_args = (wg_bf, wu_bf, wd_bf, row(ln2_g[0]), row(ln2_b[0]), alpha)
    ys = _ffn(h_s, *ffn_args, tm=ROW_TILE, tf=FFN_TILE)
    yp = _ffn(h_p, *ffn_args, tm=ROW_TILE, tf=FFN_TILE)

    return (yp.reshape(bp, seq, d), ys.reshape(bs, dseq, d),
            tail_p[None], st_p[None].astype(x_prompt.dtype),
            tail_s[None].astype(cache_conv.dtype), st_s[None].astype(state_hgrn.dtype))
```

```python
import functools

import jax
import jax.numpy as jnp
from jax import lax
from jax.experimental import pallas as pl
from jax.experimental.pallas import tpu as pltpu

LN_EPS = 1e-5
LOG2E = 1.4426950408889634
HG_HEAD_DIM = 128
HG_BLOCK = 16
SUBLANES = 8
CONV_HALO = 32
VMEM_LIMIT = 56 * 1024 * 1024
VMEM_LIMIT_DENSE = 60 * 1024 * 1024

ROW_TILE = 512
IN_PROJ_PROMPT_ROWS = 1024
IN_PROJ_PROMPT_COLS = 2048
IN_PROJ_CAST_COLS = 512
CONV_PROMPT_ROWS = 256
CONV_SAMPLE_STREAMS = 4
HGRN_PROMPT_ROWS = 512
HGRN_PROMPT_STREAMS = 1
HGRN_SAMPLE_STREAMS = 8
OUT_PROJ_SUB_ROWS = 128
FFN_TILE = 512
FFN_PROMPT_ROWS = 1024

F32 = jnp.float32
BF16 = jnp.bfloat16


def _sigmoid(x):
    return 0.5 * jnp.tanh(0.5 * x) + 0.5


def _silu(x):
    hx = 0.5 * x
    return hx * jnp.tanh(hx) + hx


def _layer_norm(x, g, b):
    mu = jnp.mean(x, axis=-1, keepdims=True)
    xc = x - mu
    var = jnp.mean(xc * xc, axis=-1, keepdims=True)
    return xc * lax.rsqrt(var + LN_EPS) * g + b


def _params(sem, vmem_limit=VMEM_LIMIT):
    return pltpu.CompilerParams(dimension_semantics=sem, vmem_limit_bytes=vmem_limit)


def _in_proj_kernel(x_ref, w_ref, b_ref, o_ref, *w_bf_ref, emit_bf16):
    w = w_ref[...].astype(BF16)
    if emit_bf16:
        w_bf_ref[0][...] = w
    o_ref[...] = jnp.dot(x_ref[...].astype(BF16), w, preferred_element_type=F32) + b_ref[...]


def _in_proj(x, w, b, *, tm, tn, emit_bf16):
    m, d = x.shape
    n = w.shape[1]
    w_spec = pl.BlockSpec((d, tn), lambda i, j: (0, j))
    out_specs = [pl.BlockSpec((tm, tn), lambda i, j: (i, j))]
    out_shape = [jax.ShapeDtypeStruct((m, n), F32)]
    if emit_bf16:
        assert m == tm, "the bf16 weight copy is written once per column tile"
        out_specs.append(w_spec)
        out_shape.append(jax.ShapeDtypeStruct((d, n), BF16))
    return pl.pallas_call(
        functools.partial(_in_proj_kernel, emit_bf16=emit_bf16),
        grid=(m // tm, n // tn),
        in_specs=[pl.BlockSpec((tm, d), lambda i, j: (i, 0)), w_spec,
                  pl.BlockSpec((1, tn), lambda i, j: (0, j))],
        out_specs=out_specs,
        out_shape=out_shape,
        compiler_params=_params(("arbitrary", "arbitrary"), VMEM_LIMIT_DENSE),
        name="in_proj",
    )(x, w, b)


SHIFT_CHUNK = 40


def _conv_kernel(a_ref, gt_ref, hist_ref, w_ref, b_ref, ng_ref, nb_ref, *rest, tt, nt, sps, taps, rows_sub,
                 n_cast):
    cast_in = rest[:n_cast]
    out_ref, tail_ref = rest[n_cast:n_cast + 2]
    cast_out = rest[n_cast + 2:2 * n_cast + 2]
    sh_ref = rest[-1]
    t = pl.program_id(1)
    lead = CONV_HALO - (taps - 1)
    n_ch = w_ref.shape[-1]

    for src, dst in zip(cast_in, cast_out):
        dst[...] = src[...].astype(BF16)

    for n in range(sps):
        rows0 = n * tt
        sh = sh_ref.at[n]

        @pl.when(t == 0)
        def _():
            sh[0, 0:SUBLANES, :] = jnp.zeros((SUBLANES, n_ch), F32)
            sh[0, lead:CONV_HALO, :] = hist_ref[n]

        sh[0, CONV_HALO:CONV_HALO + tt, :] = (a_ref[rows0:rows0 + tt, :]
                                              * _sigmoid(gt_ref[rows0:rows0 + tt, :]))

        n_shift = CONV_HALO + tt - SUBLANES
        for c0 in range(0, n_shift, SHIFT_CHUNK):
            chunk = min(SHIFT_CHUNK, n_shift - c0)
            window = sh[0, c0:c0 + chunk + SUBLANES, :]
            for p in range(1, SUBLANES):
                sh[p, c0:c0 + chunk, :] = pltpu.roll(window, chunk + SUBLANES - p, axis=0)[:chunk]

        for base in range(0, tt, rows_sub):
            acc = None
            for j in range(taps):
                phase = (lead + j) % SUBLANES
                lo = base + (lead + j) - phase
                rows = sh[phase, lo:lo + rows_sub, :].reshape(rows_sub // SUBLANES, SUBLANES, n_ch)
                term = rows * w_ref[j][None]
                acc = term if acc is None else acc + term
            hn = _layer_norm(acc.reshape(rows_sub, n_ch) + b_ref[...], ng_ref[...], nb_ref[...])
            out_ref[rows0 + base:rows0 + base + rows_sub, :] = _silu(hn).astype(out_ref.dtype)

        @pl.when(t == nt - 1)
        def _():
            tail_ref[n] = sh[0, tt + lead:tt + CONV_HALO, :]

        if nt > 1:
            sh[0, 0:CONV_HALO, :] = sh[0, tt:tt + CONV_HALO, :]


def _conv_branch(proj, hist, w_dw, b_dw, ng, nb, *, nb_batch, seq, tt, sps=1, cast=()):
    taps, c = w_dw.shape
    nt = seq // tt
    sps = sps if nt == 1 else 1
    n_groups = nb_batch // sps
    n_steps = n_groups * nt
    rows_sub = min(tt, 32)
    w_rep = jnp.broadcast_to(w_dw[:, None, :], (taps, SUBLANES, c))
    row_idx = lambda b, t: (b * nt + t, 0)
    fixed = lambda b, t: (0, 0)
    seq_blk = pl.BlockSpec((sps, taps - 1, c), lambda b, t: (b, 0, 0))
    cast_specs = [pl.BlockSpec((w.shape[0] // n_steps, w.shape[1]), row_idx) for w in cast]
    assert all(w.shape[0] % (n_steps * 2 * SUBLANES) == 0 for w in cast)
    outs = pl.pallas_call(
        functools.partial(_conv_kernel, tt=tt, nt=nt, sps=sps, taps=taps, rows_sub=rows_sub,
                          n_cast=len(cast)),
        grid=(n_groups, nt),
        in_specs=[
            pl.BlockSpec((sps * tt, c), row_idx),
            pl.BlockSpec((sps * tt, c), lambda b, t: (b * nt + t, 1)),
            seq_blk,
            pl.BlockSpec((taps, SUBLANES, c), lambda b, t: (0, 0, 0)),
            pl.BlockSpec((1, c), fixed),
            pl.BlockSpec((1, c), fixed),
            pl.BlockSpec((1, c), fixed),
        ] + cast_specs,
        out_specs=[pl.BlockSpec((sps * tt, c), row_idx), seq_blk] + cast_specs,
        out_shape=[
            jax.ShapeDtypeStruct((nb_batch * seq, c), BF16),
            jax.ShapeDtypeStruct((nb_batch, taps - 1, c), F32),
        ] + [jax.ShapeDtypeStruct(w.shape, BF16) for w in cast],
        scratch_shapes=[pltpu.VMEM((sps, SUBLANES, CONV_HALO + tt, c), F32)],
        compiler_params=_params(("arbitrary", "arbitrary")),
        name="conv_branch",
    )(proj, proj, hist, w_rep, b_dw, ng, nb, *cast)
    return outs[0], outs[1], outs[2:]


def _split3(x):
    hi = x.astype(BF16)
    r1 = x - hi.astype(F32)
    mid = r1.astype(BF16)
    lo = (r1 - mid.astype(F32)).astype(BF16)
    return hi, mid, lo


def _hgrn_kernel(q_ref, f_ref, v_ref, g_ref, s0_ref, lbp_ref, ng_ref, out_ref, sout_ref,
                 st_ref, lc_s, lck_s, q_s, v_s, o_s, *, tt, nt, heads, chunk, n_streams):
    t = pl.program_id(1)
    hd = HG_HEAD_DIM
    half = HG_BLOCK // 2
    rows_all = n_streams * tt
    width = heads * hd

    def tile(ref):
        return ref[...].reshape(rows_all, width)

    @pl.when(t == 0)
    def _():
        for n in range(n_streams):
            for h in range(heads):
                st_ref[n * heads + h] = s0_ref[n, h].T

    lbp = lbp_ref[...]
    e = jnp.exp(lbp - jnp.max(lbp, axis=0, keepdims=True))
    lb = e[0:1, :] / jnp.sum(e, axis=0, keepdims=True)

    def to_heads(dst, rows, val):
        for h in range(heads):
            dst[h, rows, :] = val[:, h * hd:(h + 1) * hd]

    k = (0.5 * (1.0 - lb)) * (1.0 - jnp.tanh(0.5 * tile(f_ref)))
    log2f = jnp.log(1.0 - k) * LOG2E
    log2k = jnp.log(k) * LOG2E
    to_heads(q_s, slice(0, rows_all), _silu(tile(q_ref)))
    to_heads(v_s, slice(0, rows_all), tile(v_ref))

    ri = lax.broadcasted_iota(jnp.int32, (chunk, chunk), 0)
    ci = lax.broadcasted_iota(jnp.int32, (chunk, chunk), 1)
    same_block_causal = (ci <= ri) & ((ri - ci) <= (ri & (HG_BLOCK - 1)))
    tri = jnp.where(same_block_causal, 1.0, 0.0).astype(BF16)
    for c in range(rows_all // chunk):
        rows = slice(c * chunk, (c + 1) * chunk)
        hi, mid, lo = _split3(log2f[rows, :])
        lc = (jnp.dot(tri, hi, preferred_element_type=F32)
              + jnp.dot(tri, mid, preferred_element_type=F32)
              + jnp.dot(tri, lo, preferred_element_type=F32))
        to_heads(lc_s, rows, lc)
        to_heads(lck_s, rows, lc - log2k[rows, :])

    row_id = lax.broadcasted_iota(jnp.int32, (half, hd), 0)

    def block_body(stream, blk):
        r0 = stream * tt + blk * HG_BLOCK
        st_base = stream * heads
        for h in range(heads):
            cs = slice(h * hd, (h + 1) * hd)
            lc = lc_s[h, pl.ds(r0, HG_BLOCK), :]
            lck = lck_s[h, pl.ds(r0, HG_BLOCK), :]
            q = q_s[h, pl.ds(r0, HG_BLOCK), :]
            v = v_s[h, pl.ds(r0, HG_BLOCK), :]
            lc_last = lc[HG_BLOCK - 1:HG_BLOCK, :]
            q_dec = q * jnp.exp2(lc)
            k_end = jnp.exp2(lc_last - lck)
            g_blk = jnp.exp2(lc_last)
            st = st_ref[st_base + h]
            o = lax.dot_general(q_dec.astype(BF16), st.astype(BF16), (((1,), (1,)), ((), ())),
                                preferred_element_type=F32)
            q_top, q_bot = q[:half], q[half:]
            lc_top, lc_bot = lc[:half], lc[half:]
            o_top = None
            o_bot = None
            for s in range(HG_BLOCK):
                c = lck_s[h, pl.ds(r0 + s, 1), :]
                vs = v_s[h, pl.ds(r0 + s, 1), :]
                if s < half:
                    top = q_top * jnp.exp2(lc_top - c)
                    if s > 0:
                        top = jnp.where(row_id >= s, top, 0.0)
                    top = jnp.sum(top, axis=-1, keepdims=True) * vs
                    o_top = top if o_top is None else o_top + top
                bot = q_bot * jnp.exp2(lc_bot - c)
                if s > half:
                    bot = jnp.where(row_id >= s - half, bot, 0.0)
                bot = jnp.sum(bot, axis=-1, keepdims=True) * vs
                o_bot = bot if o_bot is None else o_bot + bot
            o_s[pl.ds(r0, HG_BLOCK), cs] = o + jnp.concatenate([o_top, o_bot], axis=0)
            d_st = lax.dot_general(v.astype(BF16), k_end.astype(BF16), (((0,), (0,)), ((), ())),
                                   preferred_element_type=F32)
            st_ref[st_base + h] = st * g_blk + d_st

    for stream in range(n_streams):
        for blk in range(tt // HG_BLOCK):
            block_body(stream, blk)

    g_act = _silu(tile(g_ref))
    for h in range(heads):
        cs = slice(h * hd, (h + 1) * hd)
        o = o_s[:, cs]
        ms = jnp.mean(o * o, axis=-1, keepdims=True)
        on = o * lax.rsqrt(ms + LN_EPS) * ng_ref[...]
        out_ref[:, :, cs] = (on * g_act[:, cs]).astype(out_ref.dtype).reshape(n_streams, tt, hd)

    @pl.when(t == nt - 1)
    def _():
        for n in range(n_streams):
            for h in range(heads):
                sout_ref[n, h] = st_ref[n * heads + h].T


def _hgrn_branch(proj, s0, lbp, ng, *, nb_batch, seq, tt, n_streams, col0):
    w = lbp.shape[1]
    heads = w // HG_HEAD_DIM
    nt = seq // tt
    rows_all = n_streams * tt
    chunk = min(rows_all, 128)
    proj3 = proj.reshape(nb_batch, seq, proj.shape[1])
    col = lambda k: pl.BlockSpec((n_streams, tt, w), lambda g, t: (g, t, col0 + k))
    state_spec = pl.BlockSpec((n_streams, heads, HG_HEAD_DIM, HG_HEAD_DIM), lambda g, t: (g, 0, 0, 0))
    slab = pltpu.VMEM((heads, rows_all, HG_HEAD_DIM), F32)
    out, state = pl.pallas_call(
        functools.partial(_hgrn_kernel, tt=tt, nt=nt, heads=heads, chunk=chunk, n_streams=n_streams),
        grid=(nb_batch // n_streams, nt),
        in_specs=[
            col(0), col(1), col(2), col(3),
            state_spec,
            pl.BlockSpec(lbp.shape, lambda g, t: (0, 0)),
            pl.BlockSpec((1, HG_HEAD_DIM), lambda g, t: (0, 0)),
        ],
        out_specs=[
            pl.BlockSpec((n_streams, tt, w), lambda g, t: (g, t, 0)),
            state_spec,
        ],
        out_shape=[
            jax.ShapeDtypeStruct((nb_batch, seq, w), BF16),
            jax.ShapeDtypeStruct((nb_batch, heads, HG_HEAD_DIM, HG_HEAD_DIM), F32),
        ],
        scratch_shapes=[
            pltpu.VMEM((n_streams * heads, HG_HEAD_DIM, HG_HEAD_DIM), F32),
            slab, slab, slab, slab,
            pltpu.VMEM((rows_all, w), F32),
        ],
        compiler_params=_params(("arbitrary", "arbitrary")),
        name="hgrn_branch",
    )(proj3, proj3, proj3, proj3, s0, lbp, ng)
    return out.reshape(nb_batch * seq, w), state


def _out_proj_kernel(x_ref, c_ref, h_ref, wc_ref, wh_ref, g_ref, b_ref, o_ref, *, alpha, rows_sub):
    for r0 in range(0, x_ref.shape[0], rows_sub):
        rows = slice(r0, r0 + rows_sub)
        mixed = (jnp.dot(c_ref[rows, :], wc_ref[...], preferred_element_type=F32)
                 + jnp.dot(h_ref[rows, :], wh_ref[...], preferred_element_type=F32))
        o_ref[rows, :] = _layer_norm(alpha * x_ref[rows, :] + mixed, g_ref[...], b_ref[...])


def _out_proj(x, conv, hg, w_bf, g, b, alpha, *, tm, rows_sub=OUT_PROJ_SUB_ROWS):
    m, d = x.shape
    c = conv.shape[1]
    row = lambda i: (i, 0)
    fixed = lambda i: (0, 0)
    return pl.pallas_call(
        functools.partial(_out_proj_kernel, alpha=alpha, rows_sub=rows_sub),
        grid=(m // tm,),
        in_specs=[
            pl.BlockSpec((tm, d), row),
            pl.BlockSpec((tm, c), row),
            pl.BlockSpec((tm, c), row),
            pl.BlockSpec((c, d), fixed),
            pl.BlockSpec((c, d), lambda i: (1, 0)),
            pl.BlockSpec((1, d), fixed),
            pl.BlockSpec((1, d), fixed),
        ],
        out_specs=pl.BlockSpec((tm, d), row),
        out_shape=jax.ShapeDtypeStruct((m, d), F32),
        compiler_params=_params(("arbitrary",)),
        name="out_proj",
    )(x, conv, hg, w_bf, w_bf, g, b)


def _ffn_kernel(h_ref, wg_ref, wu_ref, wd_ref, g_ref, b_ref, y_ref, *, n_ff_tiles, alpha):
    j = pl.program_id(1)

    hb = h_ref[...].astype(BF16)
    gate = jnp.dot(hb, wg_ref[...], preferred_element_type=F32)
    up = jnp.dot(hb, wu_ref[...], preferred_element_type=F32)
    act = (_silu(gate) * up).astype(BF16)
    acc = jnp.where(j == 0, 0.0, y_ref[...])
    y_ref[...] = acc + jnp.dot(act, wd_ref[...], preferred_element_type=F32)

    @pl.when(j == n_ff_tiles - 1)
    def _():
        y_ref[...] = _layer_norm(alpha * h_ref[...] + y_ref[...], g_ref[...], b_ref[...])


def _ffn(h, wg_bf, wu_bf, wd_bf, g, b, alpha, *, tm, tf):
    m, d = h.shape
    dff = wg_bf.shape[1]
    nj = dff // tf
    up_spec = pl.BlockSpec((d, tf), lambda i, j: (0, j))
    fixed = lambda i, j: (0, 0)
    return pl.pallas_call(
        functools.partial(_ffn_kernel, n_ff_tiles=nj, alpha=alpha),
        grid=(m // tm, nj),
        in_specs=[
            pl.BlockSpec((tm, d), lambda i, j: (i, 0)),
            up_spec, up_spec,
            pl.BlockSpec((tf, d), lambda i, j: (j, 0)),
            pl.BlockSpec((1, d), fixed),
            pl.BlockSpec((1, d), fixed),
        ],
        out_specs=pl.BlockSpec((tm, d), lambda i, j: (i, 0)),
        out_shape=jax.ShapeDtypeStruct((m, d), F32),
        compiler_params=_params(("arbitrary", "arbitrary"), VMEM_LIMIT_DENSE),
        name="ffn",
    )(h, wg_bf, wu_bf, wd_bf, g, b)


def kernel(x_prompt, x_sample, cache_conv, state_hgrn, w_in, b_in, w_dw, b_dw, conv_norm_g, conv_norm_b,
           hg_lower_bounds, hg_norm_g, w_out, ln1_g, ln1_b, w_gate, w_up, w_down, ln2_g, ln2_b):
    depth = w_in.shape[0]
    assert depth == 1, "single-layer step"
    bp, seq, d = x_prompt.shape
    bs, dseq, _ = x_sample.shape
    conv_w = w_dw.shape[2]
    conv_state = w_dw.shape[1] - 1
    hg_w = hg_lower_bounds.shape[1]
    heads = hg_w // HG_HEAD_DIM
    assert conv_w == hg_w and w_in.shape[2] == 2 * conv_w + 4 * hg_w
    assert bs * dseq == ROW_TILE, "the sample rows form one dense-stage row tile"
    alpha = (2.0 * depth) ** 0.25

    xp = x_prompt.reshape(bp * seq, d)
    xs = x_sample.reshape(bs * dseq, d)
    row = lambda a: a.reshape(1, -1)

    proj_s, w_in_bf = _in_proj(xs, w_in[0], row(b_in[0]), tm=ROW_TILE, tn=IN_PROJ_CAST_COLS, emit_bf16=True)
    (proj_p,) = _in_proj(xp, w_in_bf, row(b_in[0]), tm=IN_PROJ_PROMPT_ROWS, tn=IN_PROJ_PROMPT_COLS,
                         emit_bf16=False)

    hist_p = jnp.zeros((bp, conv_state, conv_w), F32)
    conv_args = (w_dw[0], row(b_dw[0]), row(conv_norm_g[0]), row(conv_norm_b[0]))
    conv_p, tail_p, (w_out_bf, wg_bf, wu_bf, wd_bf) = _conv_branch(
        proj_p, hist_p, *conv_args, nb_batch=bp, seq=seq, tt=CONV_PROMPT_ROWS,
        cast=(w_out[0], w_gate[0], w_up[0], w_down[0]))
    conv_s, tail_s, _ = _conv_branch(proj_s, cache_conv[0], *conv_args, nb_batch=bs, seq=dseq, tt=dseq,
                                     sps=CONV_SAMPLE_STREAMS)

    s0_p = jnp.zeros((bp, heads, HG_HEAD_DIM, HG_HEAD_DIM), F32)
    hg_args = (hg_lower_bounds, row(hg_norm_g[0]))
    hg_s, st_s = _hgrn_branch(proj_s, state_hgrn[0], *hg_args, nb_batch=bs, seq=dseq, tt=dseq,
                              n_streams=HGRN_SAMPLE_STREAMS, col0=2)
    hg_p, st_p = _hgrn_branch(proj_p, s0_p, *hg_args, nb_batch=bp, seq=seq, tt=HGRN_PROMPT_ROWS,
                              n_streams=HGRN_PROMPT_STREAMS, col0=2)

    out_args = (w_out_bf, row(ln1_g[0]), row(ln1_b[0]), alpha)
    h_s = _out_proj(xs, conv_s, hg_s, *out_args, tm=ROW_TILE)
    h_p = _out_proj(xp, conv_p, hg_p, *out_args, tm=ROW_TILE)

    ffn_args = (wg_bf, wu_bf, wd_bf, row(ln2_g[0]), row(ln2_b[0]), alpha)
    ys = _ffn(h_s, *ffn_args, tm=ROW_TILE, tf=FFN_TILE)
    yp = _ffn(h_p, *ffn_args, tm=FFN_PROMPT_ROWS, tf=FFN_TILE)

    return (yp.reshape(bp, seq, d), ys.reshape(bs, dseq, d),
            tail_p[None], st_p[None].astype(x_prompt.dtype),
            tail_s[None].astype(cache_conv.dtype), st_s[None].astype(state_hgrn.dtype))
```

```python
import functools

import jax
import jax.numpy as jnp
from jax import lax
from jax.experimental import pallas as pl
from jax.experimental.pallas import tpu as pltpu

LN_EPS = 1e-5
LOG2E = 1.4426950408889634
HG_HEAD_DIM = 128
HG_BLOCK = 16
SUBLANES = 8
CONV_HALO = 32
VMEM_LIMIT = 56 * 1024 * 1024
VMEM_LIMIT_DENSE = 60 * 1024 * 1024

ROW_TILE = 512
IN_PROJ_PROMPT_ROWS = 1024
IN_PROJ_PROMPT_COLS = 2048
IN_PROJ_CAST_COLS = 512
CONV_PROMPT_ROWS = 256
CONV_SAMPLE_STREAMS = 4
HGRN_PROMPT_ROWS = 512
HGRN_PROMPT_STREAMS = 1
HGRN_SAMPLE_STREAMS = 8
OUT_PROJ_SUB_ROWS = 256
FFN_TILE = 512
FFN_PROMPT_ROWS = 1024

F32 = jnp.float32
BF16 = jnp.bfloat16


def _sigmoid(x):
    return 0.5 * jnp.tanh(0.5 * x) + 0.5


def _silu(x):
    hx = 0.5 * x
    return hx * jnp.tanh(hx) + hx


def _layer_norm(x, g, b):
    mu = jnp.mean(x, axis=-1, keepdims=True)
    xc = x - mu
    var = jnp.mean(xc * xc, axis=-1, keepdims=True)
    return xc * lax.rsqrt(var + LN_EPS) * g + b


def _params(sem, vmem_limit=VMEM_LIMIT):
    return pltpu.CompilerParams(dimension_semantics=sem, vmem_limit_bytes=vmem_limit)


def _in_proj_kernel(x_ref, w_ref, b_ref, o_ref, *w_bf_ref, emit_bf16):
    w = w_ref[...].astype(BF16)
    if emit_bf16:
        w_bf_ref[0][...] = w
    o_ref[...] = jnp.dot(x_ref[...].astype(BF16), w, preferred_element_type=F32) + b_ref[...]


def _in_proj(x, w, b, *, tm, tn, emit_bf16):
    m, d = x.shape
    n = w.shape[1]
    w_spec = pl.BlockSpec((d, tn), lambda i, j: (0, j))
    out_specs = [pl.BlockSpec((tm, tn), lambda i, j: (i, j))]
    out_shape = [jax.ShapeDtypeStruct((m, n), F32)]
    if emit_bf16:
        assert m == tm, "the bf16 weight copy is written once per column tile"
        out_specs.append(w_spec)
        out_shape.append(jax.ShapeDtypeStruct((d, n), BF16))
    return pl.pallas_call(
        functools.partial(_in_proj_kernel, emit_bf16=emit_bf16),
        grid=(m // tm, n // tn),
        in_specs=[pl.BlockSpec((tm, d), lambda i, j: (i, 0)), w_spec,
                  pl.BlockSpec((1, tn), lambda i, j: (0, j))],
        out_specs=out_specs,
        out_shape=out_shape,
        compiler_params=_params(("arbitrary", "arbitrary"), VMEM_LIMIT_DENSE),
        name="in_proj",
    )(x, w, b)


SHIFT_CHUNK = 40


def _conv_kernel(a_ref, gt_ref, hist_ref, w_ref, b_ref, ng_ref, nb_ref, *rest, tt, nt, sps, taps, rows_sub,
                 n_cast):
    cast_in = rest[:n_cast]
    out_ref, tail_ref = rest[n_cast:n_cast + 2]
    cast_out = rest[n_cast + 2:2 * n_cast + 2]
    sh_ref = rest[-1]
    t = pl.program_id(1)
    lead = CONV_HALO - (taps - 1)
    n_ch = w_ref.shape[-1]

    for src, dst in zip(cast_in, cast_out):
        dst[...] = src[...].astype(BF16)

    for n in range(sps):
        rows0 = n * tt
        sh = sh_ref.at[n]

        @pl.when(t == 0)
        def _():
            sh[0, 0:SUBLANES, :] = jnp.zeros((SUBLANES, n_ch), F32)
            sh[0, lead:CONV_HALO, :] = hist_ref[n]

        sh[0, CONV_HALO:CONV_HALO + tt, :] = (a_ref[rows0:rows0 + tt, :]
                                              * _sigmoid(gt_ref[rows0:rows0 + tt, :]))

        n_shift = CONV_HALO + tt - SUBLANES
        for c0 in range(0, n_shift, SHIFT_CHUNK):
            chunk = min(SHIFT_CHUNK, n_shift - c0)
            window = sh[0, c0:c0 + chunk + SUBLANES, :]
            for p in range(1, SUBLANES):
                sh[p, c0:c0 + chunk, :] = pltpu.roll(window, chunk + SUBLANES - p, axis=0)[:chunk]

        for base in range(0, tt, rows_sub):
            acc = None
            for j in range(taps):
                phase = (lead + j) % SUBLANES
                lo = base + (lead + j) - phase
                rows = sh[phase, lo:lo + rows_sub, :].reshape(rows_sub // SUBLANES, SUBLANES, n_ch)
                term = rows * w_ref[j][None]
                acc = term if acc is None else acc + term
            hn = _layer_norm(acc.reshape(rows_sub, n_ch) + b_ref[...], ng_ref[...], nb_ref[...])
            out_ref[rows0 + base:rows0 + base + rows_sub, :] = _silu(hn).astype(out_ref.dtype)

        @pl.when(t == nt - 1)
        def _():
            tail_ref[n] = sh[0, tt + lead:tt + CONV_HALO, :]

        if nt > 1:
            sh[0, 0:CONV_HALO, :] = sh[0, tt:tt + CONV_HALO, :]


def _conv_branch(proj, hist, w_dw, b_dw, ng, nb, *, nb_batch, seq, tt, sps=1, cast=()):
    taps, c = w_dw.shape
    nt = seq // tt
    sps = sps if nt == 1 else 1
    n_groups = nb_batch // sps
    n_steps = n_groups * nt
    rows_sub = min(tt, 32)
    w_rep = jnp.broadcast_to(w_dw[:, None, :], (taps, SUBLANES, c))
    row_idx = lambda b, t: (b * nt + t, 0)
    fixed = lambda b, t: (0, 0)
    seq_blk = pl.BlockSpec((sps, taps - 1, c), lambda b, t: (b, 0, 0))
    cast_specs = [pl.BlockSpec((w.shape[0] // n_steps, w.shape[1]), row_idx) for w in cast]
    assert all(w.shape[0] % (n_steps * 2 * SUBLANES) == 0 for w in cast)
    outs = pl.pallas_call(
        functools.partial(_conv_kernel, tt=tt, nt=nt, sps=sps, taps=taps, rows_sub=rows_sub,
                          n_cast=len(cast)),
        grid=(n_groups, nt),
        in_specs=[
            pl.BlockSpec((sps * tt, c), row_idx),
            pl.BlockSpec((sps * tt, c), lambda b, t: (b * nt + t, 1)),
            seq_blk,
            pl.BlockSpec((taps, SUBLANES, c), lambda b, t: (0, 0, 0)),
            pl.BlockSpec((1, c), fixed),
            pl.BlockSpec((1, c), fixed),
            pl.BlockSpec((1, c), fixed),
        ] + cast_specs,
        out_specs=[pl.BlockSpec((sps * tt, c), row_idx), seq_blk] + cast_specs,
        out_shape=[
            jax.ShapeDtypeStruct((nb_batch * seq, c), BF16),
            jax.ShapeDtypeStruct((nb_batch, taps - 1, c), F32),
        ] + [jax.ShapeDtypeStruct(w.shape, BF16) for w in cast],
        scratch_shapes=[pltpu.VMEM((sps, SUBLANES, CONV_HALO + tt, c), F32)],
        compiler_params=_params(("arbitrary", "arbitrary")),
        name="conv_branch",
    )(proj, proj, hist, w_rep, b_dw, ng, nb, *cast)
    return outs[0], outs[1], outs[2:]


def _split3(x):
    hi = x.astype(BF16)
    r1 = x - hi.astype(F32)
    mid = r1.astype(BF16)
    lo = (r1 - mid.astype(F32)).astype(BF16)
    return hi, mid, lo


def _hgrn_kernel(q_ref, f_ref, v_ref, g_ref, s0_ref, lbp_ref, ng_ref, out_ref, sout_ref,
                 st_ref, lc_s, lck_s, q_s, v_s, o_s, *, tt, nt, heads, chunk, n_streams):
    t = pl.program_id(1)
    hd = HG_HEAD_DIM
    half = HG_BLOCK // 2
    rows_all = n_streams * tt
    width = heads * hd

    def tile(ref):
        return ref[...].reshape(rows_all, width)

    @pl.when(t == 0)
    def _():
        for n in range(n_streams):
            for h in range(heads):
                st_ref[n * heads + h] = s0_ref[n, h].T

    lbp = lbp_ref[...]
    e = jnp.exp(lbp - jnp.max(lbp, axis=0, keepdims=True))
    lb = e[0:1, :] / jnp.sum(e, axis=0, keepdims=True)

    def to_heads(dst, rows, val):
        for h in range(heads):
            dst[h, rows, :] = val[:, h * hd:(h + 1) * hd]

    k = (0.5 * (1.0 - lb)) * (1.0 - jnp.tanh(0.5 * tile(f_ref)))
    log2f = jnp.log(1.0 - k) * LOG2E
    log2k = jnp.log(k) * LOG2E
    to_heads(q_s, slice(0, rows_all), _silu(tile(q_ref)))
    to_heads(v_s, slice(0, rows_all), tile(v_ref))

    ri = lax.broadcasted_iota(jnp.int32, (chunk, chunk), 0)
    ci = lax.broadcasted_iota(jnp.int32, (chunk, chunk), 1)
    same_block_causal = (ci <= ri) & ((ri - ci) <= (ri & (HG_BLOCK - 1)))
    tri = jnp.where(same_block_causal, 1.0, 0.0).astype(BF16)
    for c in range(rows_all // chunk):
        rows = slice(c * chunk, (c + 1) * chunk)
        hi, mid, lo = _split3(log2f[rows, :])
        lc = (jnp.dot(tri, hi, preferred_element_type=F32)
              + jnp.dot(tri, mid, preferred_element_type=F32)
              + jnp.dot(tri, lo, preferred_element_type=F32))
        to_heads(lc_s, rows, lc)
        to_heads(lck_s, rows, lc - log2k[rows, :])

    row_id = lax.broadcasted_iota(jnp.int32, (half, hd), 0)

    def block_body(stream, blk):
        r0 = stream * tt + blk * HG_BLOCK
        st_base = stream * heads
        for h in range(heads):
            cs = slice(h * hd, (h + 1) * hd)
            lc = lc_s[h, pl.ds(r0, HG_BLOCK), :]
            lck = lck_s[h, pl.ds(r0, HG_BLOCK), :]
            q = q_s[h, pl.ds(r0, HG_BLOCK), :]
            v = v_s[h, pl.ds(r0, HG_BLOCK), :]
            lc_last = lc[HG_BLOCK - 1:HG_BLOCK, :]
            q_dec = q * jnp.exp2(lc)
            k_end = jnp.exp2(lc_last - lck)
            g_blk = jnp.exp2(lc_last)
            st = st_ref[st_base + h]
            o = lax.dot_general(q_dec.astype(BF16), st.astype(BF16), (((1,), (1,)), ((), ())),
                                preferred_element_type=F32)
            q_top, q_bot = q[:half], q[half:]
            lc_top, lc_bot = lc[:half], lc[half:]
            o_top = None
            o_bot = None
            for s in range(HG_BLOCK):
                c = lck_s[h, pl.ds(r0 + s, 1), :]
                vs = v_s[h, pl.ds(r0 + s, 1), :]
                if s < half:
                    top = q_top * jnp.exp2(lc_top - c)
                    if s > 0:
                        top = jnp.where(row_id >= s, top, 0.0)
                    top = jnp.sum(top, axis=-1, keepdims=True) * vs
                    o_top = top if o_top is None else o_top + top
                bot = q_bot * jnp.exp2(lc_bot - c)
                if s > half:
                    bot = jnp.where(row_id >= s - half, bot, 0.0)
                bot = jnp.sum(bot, axis=-1, keepdims=True) * vs
                o_bot = bot if o_bot is None else o_bot + bot
            o_s[pl.ds(r0, HG_BLOCK), cs] = o + jnp.concatenate([o_top, o_bot], axis=0)
            d_st = lax.dot_general(v.astype(BF16), k_end.astype(BF16), (((0,), (0,)), ((), ())),
                                   preferred_element_type=F32)
            st_ref[st_base + h] = st * g_blk + d_st

    for stream in range(n_streams):
        for blk in range(tt // HG_BLOCK):
            block_body(stream, blk)

    g_act = _silu(tile(g_ref))
    for h in range(heads):
        cs = slice(h * hd, (h + 1) * hd)
        o = o_s[:, cs]
        ms = jnp.mean(o * o, axis=-1, keepdims=True)
        on = o * lax.rsqrt(ms + LN_EPS) * ng_ref[...]
        out_ref[:, :, cs] = (on * g_act[:, cs]).astype(out_ref.dtype).reshape(n_streams, tt, hd)

    @pl.when(t == nt - 1)
    def _():
        for n in range(n_streams):
            for h in range(heads):
                sout_ref[n, h] = st_ref[n * heads + h].T


def _hgrn_branch(proj, s0, lbp, ng, *, nb_batch, seq, tt, n_streams, col0):
    w = lbp.shape[1]
    heads = w // HG_HEAD_DIM
    nt = seq // tt
    rows_all = n_streams * tt
    chunk = min(rows_all, 128)
    proj3 = proj.reshape(nb_batch, seq, proj.shape[1])
    col = lambda k: pl.BlockSpec((n_streams, tt, w), lambda g, t: (g, t, col0 + k))
    state_spec = pl.BlockSpec((n_streams, heads, HG_HEAD_DIM, HG_HEAD_DIM), lambda g, t: (g, 0, 0, 0))
    slab = pltpu.VMEM((heads, rows_all, HG_HEAD_DIM), F32)
    out, state = pl.pallas_call(
        functools.partial(_hgrn_kernel, tt=tt, nt=nt, heads=heads, chunk=chunk, n_streams=n_streams),
        grid=(nb_batch // n_streams, nt),
        in_specs=[
            col(0), col(1), col(2), col(3),
            state_spec,
            pl.BlockSpec(lbp.shape, lambda g, t: (0, 0)),
            pl.BlockSpec((1, HG_HEAD_DIM), lambda g, t: (0, 0)),
        ],
        out_specs=[
            pl.BlockSpec((n_streams, tt, w), lambda g, t: (g, t, 0)),
            state_spec,
        ],
        out_shape=[
            jax.ShapeDtypeStruct((nb_batch, seq, w), BF16),
            jax.ShapeDtypeStruct((nb_batch, heads, HG_HEAD_DIM, HG_HEAD_DIM), F32),
        ],
        scratch_shapes=[
            pltpu.VMEM((n_streams * heads, HG_HEAD_DIM, HG_HEAD_DIM), F32),
            slab, slab, slab, slab,
            pltpu.VMEM((rows_all, w), F32),
        ],
        compiler_params=_params(("arbitrary", "arbitrary")),
        name="hgrn_branch",
    )(proj3, proj3, proj3, proj3, s0, lbp, ng)
    return out.reshape(nb_batch * seq, w), state


def _out_proj_kernel(x_ref, c_ref, h_ref, wc_ref, wh_ref, g_ref, b_ref, o_ref, *, alpha, rows_sub):
    for r0 in range(0, x_ref.shape[0], rows_sub):
        rows = slice(r0, r0 + rows_sub)
        mixed = (jnp.dot(c_ref[rows, :], wc_ref[...], preferred_element_type=F32)
                 + jnp.dot(h_ref[rows, :], wh_ref[...], preferred_element_type=F32))
        o_ref[rows, :] = _layer_norm(alpha * x_ref[rows, :] + mixed, g_ref[...], b_ref[...])


def _out_proj(x, conv, hg, w_bf, g, b, alpha, *, tm, rows_sub=OUT_PROJ_SUB_ROWS):
    m, d = x.shape
    c = conv.shape[1]
    row = lambda i: (i, 0)
    fixed = lambda i: (0, 0)
    return pl.pallas_call(
        functools.partial(_out_proj_kernel, alpha=alpha, rows_sub=rows_sub),
        grid=(m // tm,),
        in_specs=[
            pl.BlockSpec((tm, d), row),
            pl.BlockSpec((tm, c), row),
            pl.BlockSpec((tm, c), row),
            pl.BlockSpec((c, d), fixed),
            pl.BlockSpec((c, d), lambda i: (1, 0)),
            pl.BlockSpec((1, d), fixed),
            pl.BlockSpec((1, d), fixed),
        ],
        out_specs=pl.BlockSpec((tm, d), row),
        out_shape=jax.ShapeDtypeStruct((m, d), F32),
        compiler_params=_params(("arbitrary",)),
        name="out_proj",
    )(x, conv, hg, w_bf, w_bf, g, b)


def _ffn_kernel(h_ref, wg_ref, wu_ref, wd_ref, g_ref, b_ref, y_ref, *, n_ff_tiles, alpha):
    j = pl.program_id(1)

    hb = h_ref[...].astype(BF16)
    gate = jnp.dot(hb, wg_ref[...], preferred_element_type=F32)
    up = jnp.dot(hb, wu_ref[...], preferred_element_type=F32)
    act = (_silu(gate) * up).astype(BF16)
    acc = jnp.where(j == 0, 0.0, y_ref[...])
    y_ref[...] = acc + jnp.dot(act, wd_ref[...], preferred_element_type=F32)

    @pl.when(j == n_ff_tiles - 1)
    def _():
        y_ref[...] = _layer_norm(alpha * h_ref[...] + y_ref[...], g_ref[...], b_ref[...])


def _ffn(h, wg_bf, wu_bf, wd_bf, g, b, alpha, *, tm, tf):
    m, d = h.shape
    dff = wg_bf.shape[1]
    nj = dff // tf
    up_spec = pl.BlockSpec((d, tf), lambda i, j: (0, j))
    fixed = lambda i, j: (0, 0)
    return pl.pallas_call(
        functools.partial(_ffn_kernel, n_ff_tiles=nj, alpha=alpha),
        grid=(m // tm, nj),
        in_specs=[
            pl.BlockSpec((tm, d), lambda i, j: (i, 0)),
            up_spec, up_spec,
            pl.BlockSpec((tf, d), lambda i, j: (j, 0)),
            pl.BlockSpec((1, d), fixed),
            pl.BlockSpec((1, d), fixed),
        ],
        out_specs=pl.BlockSpec((tm, d), lambda i, j: (i, 0)),
        out_shape=jax.ShapeDtypeStruct((m, d), F32),
        compiler_params=_params(("arbitrary", "arbitrary"), VMEM_LIMIT_DENSE),
        name="ffn",
    )(h, wg_bf, wu_bf, wd_bf, g, b)


def kernel(x_prompt, x_sample, cache_conv, state_hgrn, w_in, b_in, w_dw, b_dw, conv_norm_g, conv_norm_b,
           hg_lower_bounds, hg_norm_g, w_out, ln1_g, ln1_b, w_gate, w_up, w_down, ln2_g, ln2_b):
    depth = w_in.shape[0]
    assert depth == 1, "single-layer step"
    bp, seq, d = x_prompt.shape
    bs, dseq, _ = x_sample.shape
    conv_w = w_dw.shape[2]
    conv_state = w_dw.shape[1] - 1
    hg_w = hg_lower_bounds.shape[1]
    heads = hg_w // HG_HEAD_DIM
    assert conv_w == hg_w and w_in.shape[2] == 2 * conv_w + 4 * hg_w
    assert bs * dseq == ROW_TILE, "the sample rows form one dense-stage row tile"
    alpha = (2.0 * depth) ** 0.25

    xp = x_prompt.reshape(bp * seq, d)
    xs = x_sample.reshape(bs * dseq, d)
    row = lambda a: a.reshape(1, -1)

    proj_s, w_in_bf = _in_proj(xs, w_in[0], row(b_in[0]), tm=ROW_TILE, tn=IN_PROJ_CAST_COLS, emit_bf16=True)
    (proj_p,) = _in_proj(xp, w_in_bf, row(b_in[0]), tm=IN_PROJ_PROMPT_ROWS, tn=IN_PROJ_PROMPT_COLS,
                         emit_bf16=False)

    hist_p = jnp.zeros((bp, conv_state, conv_w), F32)
    conv_args = (w_dw[0], row(b_dw[0]), row(conv_norm_g[0]), row(conv_norm_b[0]))
    conv_p, tail_p, (w_out_bf, wg_bf, wu_bf, wd_bf) = _conv_branch(
        proj_p, hist_p, *conv_args, nb_batch=bp, seq=seq, tt=CONV_PROMPT_ROWS,
        cast=(w_out[0], w_gate[0], w_up[0], w_down[0]))
    conv_s, tail_s, _ = _conv_branch(proj_s, cache_conv[0], *conv_args, nb_batch=bs, seq=dseq, tt=dseq,
                                     sps=CONV_SAMPLE_STREAMS)

    s0_p = jnp.zeros((bp, heads, HG_HEAD_DIM, HG_HEAD_DIM), F32)
    hg_args = (hg_lower_bounds, row(hg_norm_g[0]))
    hg_s, st_s = _hgrn_branch(proj_s, state_hgrn[0], *hg_args, nb_batch=bs, seq=dseq, tt=dseq,
                              n_streams=HGRN_SAMPLE_STREAMS, col0=2)
    hg_p, st_p = _hgrn_branch(proj_p, s0_p, *hg_args, nb_batch=bp, seq=seq, tt=HGRN_PROMPT_ROWS,
                              n_streams=HGRN_PROMPT_STREAMS, col0=2)

    out_args = (w_out_bf, row(ln1_g[0]), row(ln1_b[0]), alpha)
    h_s = _out_proj(xs, conv_s, hg_s, *out_args, tm=ROW_TILE)
    h_p = _out_proj(xp, conv_p, hg_p, *out_args, tm=ROW_TILE)

    ffn_args = (wg_bf, wu_bf, wd_bf, row(ln2_g[0]), row(ln2_b[0]), alpha)
    ys = _ffn(h_s, *ffn_args, tm=ROW_TILE, tf=FFN_TILE)
    yp = _ffn(h_p, *ffn_args, tm=FFN_PROMPT_ROWS, tf=FFN_TILE)

    return (yp.reshape(bp, seq, d), ys.reshape(bs, dseq, d),
            tail_p[None], st_p[None].astype(x_prompt.dtype),
            tail_s[None].astype(cache_conv.dtype), st_s[None].astype(state_hgrn.dtype))
```

```python
import functools

import jax
import jax.numpy as jnp
from jax import lax
from jax.experimental import pallas as pl
from jax.experimental.pallas import tpu as pltpu

LN_EPS = 1e-5
LOG2E = 1.4426950408889634
HG_HEAD_DIM = 128
HG_BLOCK = 16
SUBLANES = 8
CONV_HALO = 32
VMEM_LIMIT = 56 * 1024 * 1024
VMEM_LIMIT_DENSE = 60 * 1024 * 1024

ROW_TILE = 512
IN_PROJ_PROMPT_ROWS = 1024
IN_PROJ_PROMPT_COLS = 2048
IN_PROJ_CAST_ROWS = 256
CONV_PROMPT_ROWS = 256
CONV_SAMPLE_STREAMS = 4
HGRN_PROMPT_ROWS = 512
HGRN_PROMPT_STREAMS = 1
HGRN_SAMPLE_STREAMS = 8
OUT_PROJ_SUB_ROWS = 256
FFN_TILE = 512
FFN_PROMPT_ROWS = 1024

F32 = jnp.float32
BF16 = jnp.bfloat16


def _sigmoid(x):
    return 0.5 * jnp.tanh(0.5 * x) + 0.5


def _silu(x):
    hx = 0.5 * x
    return hx * jnp.tanh(hx) + hx


def _layer_norm(x, g, b):
    mu = jnp.mean(x, axis=-1, keepdims=True)
    xc = x - mu
    var = jnp.mean(xc * xc, axis=-1, keepdims=True)
    return xc * lax.rsqrt(var + LN_EPS) * g + b


def _params(sem, vmem_limit=VMEM_LIMIT):
    return pltpu.CompilerParams(dimension_semantics=sem, vmem_limit_bytes=vmem_limit)


def _in_proj_kernel(x_ref, w_ref, b_ref, o_ref):
    o_ref[...] = jnp.dot(x_ref[...].astype(BF16), w_ref[...], preferred_element_type=F32) + b_ref[...]


def _in_proj_cast_kernel(x_ref, w_ref, b_ref, o_ref, w_bf_ref):
    k = pl.program_id(0)
    w = w_ref[...].astype(BF16)
    w_bf_ref[...] = w
    acc = jnp.where(k == 0, b_ref[...], o_ref[...])
    o_ref[...] = acc + jnp.dot(x_ref[...].astype(BF16), w, preferred_element_type=F32)


def _in_proj_cast(x, w, b, *, tk):
    m, d = x.shape
    n = w.shape[1]
    w_spec = pl.BlockSpec((tk, n), lambda k: (k, 0))
    return pl.pallas_call(
        _in_proj_cast_kernel,
        grid=(d // tk,),
        in_specs=[pl.BlockSpec((m, tk), lambda k: (0, k)), w_spec, pl.BlockSpec((1, n), lambda k: (0, 0))],
        out_specs=[pl.BlockSpec((m, n), lambda k: (0, 0)), w_spec],
        out_shape=[jax.ShapeDtypeStruct((m, n), F32), jax.ShapeDtypeStruct((d, n), BF16)],
        compiler_params=_params(("arbitrary",)),
        name="in_proj_cast",
    )(x, w, b)


def _in_proj(x, w_bf, b, *, tm, tn):
    m, d = x.shape
    n = w_bf.shape[1]
    return pl.pallas_call(
        _in_proj_kernel,
        grid=(m // tm, n // tn),
        in_specs=[pl.BlockSpec((tm, d), lambda i, j: (i, 0)),
                  pl.BlockSpec((d, tn), lambda i, j: (0, j)),
                  pl.BlockSpec((1, tn), lambda i, j: (0, j))],
        out_specs=pl.BlockSpec((tm, tn), lambda i, j: (i, j)),
        out_shape=jax.ShapeDtypeStruct((m, n), F32),
        compiler_params=_params(("arbitrary", "arbitrary"), VMEM_LIMIT_DENSE),
        name="in_proj",
    )(x, w_bf, b)


SHIFT_CHUNK = 40


def _conv_kernel(a_ref, gt_ref, hist_ref, w_ref, b_ref, ng_ref, nb_ref, *rest, tt, nt, sps, taps, rows_sub,
                 n_cast):
    cast_in = rest[:n_cast]
    out_ref, tail_ref = rest[n_cast:n_cast + 2]
    cast_out = rest[n_cast + 2:2 * n_cast + 2]
    sh_ref = rest[-1]
    t = pl.program_id(1)
    lead = CONV_HALO - (taps - 1)
    n_ch = w_ref.shape[-1]

    for src, dst in zip(cast_in, cast_out):
        dst[...] = src[...].astype(BF16)

    for n in range(sps):
        rows0 = n * tt
        sh = sh_ref.at[n]

        @pl.when(t == 0)
        def _():
            sh[0, 0:SUBLANES, :] = jnp.zeros((SUBLANES, n_ch), F32)
            sh[0, lead:CONV_HALO, :] = hist_ref[n]

        sh[0, CONV_HALO:CONV_HALO + tt, :] = (a_ref[rows0:rows0 + tt, :]
                                              * _sigmoid(gt_ref[rows0:rows0 + tt, :]))

        n_shift = CONV_HALO + tt - SUBLANES
        for c0 in range(0, n_shift, SHIFT_CHUNK):
            chunk = min(SHIFT_CHUNK, n_shift - c0)
            window = sh[0, c0:c0 + chunk + SUBLANES, :]
            for p in range(1, SUBLANES):
                sh[p, c0:c0 + chunk, :] = pltpu.roll(window, chunk + SUBLANES - p, axis=0)[:chunk]

        for base in range(0, tt, rows_sub):
            acc = None
            for j in range(taps):
                phase = (lead + j) % SUBLANES
                lo = base + (lead + j) - phase
                rows = sh[phase, lo:lo + rows_sub, :].reshape(rows_sub // SUBLANES, SUBLANES, n_ch)
                term = rows * w_ref[j][None]
                acc = term if acc is None else acc + term
            hn = _layer_norm(acc.reshape(rows_sub, n_ch) + b_ref[...], ng_ref[...], nb_ref[...])
            out_ref[rows0 + base:rows0 + base + rows_sub, :] = _silu(hn).astype(out_ref.dtype)

        @pl.when(t == nt - 1)
        def _():
            tail_ref[n] = sh[0, tt + lead:tt + CONV_HALO, :]

        if nt > 1:
            sh[0, 0:CONV_HALO, :] = sh[0, tt:tt + CONV_HALO, :]


def _conv_branch(proj, hist, w_dw, b_dw, ng, nb, *, nb_batch, seq, tt, sps=1, cast=()):
    taps, c = w_dw.shape
    nt = seq // tt
    sps = sps if nt == 1 else 1
    n_groups = nb_batch // sps
    n_steps = n_groups * nt
    rows_sub = min(tt, 32)
    w_rep = jnp.broadcast_to(w_dw[:, None, :], (taps, SUBLANES, c))
    row_idx = lambda b, t: (b * nt + t, 0)
    fixed = lambda b, t: (0, 0)
    seq_blk = pl.BlockSpec((sps, taps - 1, c), lambda b, t: (b, 0, 0))
    cast_specs = [pl.BlockSpec((w.shape[0] // n_steps, w.shape[1]), row_idx) for w in cast]
    assert all(w.shape[0] % (n_steps * 2 * SUBLANES) == 0 for w in cast)
    outs = pl.pallas_call(
        functools.partial(_conv_kernel, tt=tt, nt=nt, sps=sps, taps=taps, rows_sub=rows_sub,
                          n_cast=len(cast)),
        grid=(n_groups, nt),
        in_specs=[
            pl.BlockSpec((sps * tt, c), row_idx),
            pl.BlockSpec((sps * tt, c), lambda b, t: (b * nt + t, 1)),
            seq_blk,
            pl.BlockSpec((taps, SUBLANES, c), lambda b, t: (0, 0, 0)),
            pl.BlockSpec((1, c), fixed),
            pl.BlockSpec((1, c), fixed),
            pl.BlockSpec((1, c), fixed),
        ] + cast_specs,
        out_specs=[pl.BlockSpec((sps * tt, c), row_idx), seq_blk] + cast_specs,
        out_shape=[
            jax.ShapeDtypeStruct((nb_batch * seq, c), BF16),
            jax.ShapeDtypeStruct((nb_batch, taps - 1, c), F32),
        ] + [jax.ShapeDtypeStruct(w.shape, BF16) for w in cast],
        scratch_shapes=[pltpu.VMEM((sps, SUBLANES, CONV_HALO + tt, c), F32)],
        compiler_params=_params(("arbitrary", "arbitrary")),
        name="conv_branch",
    )(proj, proj, hist, w_rep, b_dw, ng, nb, *cast)
    return outs[0], outs[1], outs[2:]


def _split3(x):
    hi = x.astype(BF16)
    r1 = x - hi.astype(F32)
    mid = r1.astype(BF16)
    lo = (r1 - mid.astype(F32)).astype(BF16)
    return hi, mid, lo


def _hgrn_kernel(q_ref, f_ref, v_ref, g_ref, s0_ref, lbp_ref, ng_ref, out_ref, sout_ref,
                 st_ref, lc_s, lck_s, q_s, v_s, o_s, *, tt, nt, heads, chunk, n_streams):
    t = pl.program_id(1)
    hd = HG_HEAD_DIM
    half = HG_BLOCK // 2
    rows_all = n_streams * tt
    width = heads * hd

    def tile(ref):
        return ref[...].reshape(rows_all, width)

    @pl.when(t == 0)
    def _():
        for n in range(n_streams):
            for h in range(heads):
                st_ref[n * heads + h] = s0_ref[n, h].T

    lbp = lbp_ref[...]
    e = jnp.exp(lbp - jnp.max(lbp, axis=0, keepdims=True))
    lb = e[0:1, :] / jnp.sum(e, axis=0, keepdims=True)

    def to_heads(dst, rows, val):
        for h in range(heads):
            dst[h, rows, :] = val[:, h * hd:(h + 1) * hd]

    k = (0.5 * (1.0 - lb)) * (1.0 - jnp.tanh(0.5 * tile(f_ref)))
    log2f = jnp.log(1.0 - k) * LOG2E
    log2k = jnp.log(k) * LOG2E
    to_heads(q_s, slice(0, rows_all), _silu(tile(q_ref)))
    to_heads(v_s, slice(0, rows_all), tile(v_ref))

    ri = lax.broadcasted_iota(jnp.int32, (chunk, chunk), 0)
    ci = lax.broadcasted_iota(jnp.int32, (chunk, chunk), 1)
    same_block_causal = (ci <= ri) & ((ri - ci) <= (ri & (HG_BLOCK - 1)))
    tri = jnp.where(same_block_causal, 1.0, 0.0).astype(BF16)
    for c in range(rows_all // chunk):
        rows = slice(c * chunk, (c + 1) * chunk)
        hi, mid, lo = _split3(log2f[rows, :])
        lc = (jnp.dot(tri, hi, preferred_element_type=F32)
              + jnp.dot(tri, mid, preferred_element_type=F32)
              + jnp.dot(tri, lo, preferred_element_type=F32))
        to_heads(lc_s, rows, lc)
        to_heads(lck_s, rows, lc - log2k[rows, :])

    row_id = lax.broadcasted_iota(jnp.int32, (half, hd), 0)

    def block_body(stream, blk):
        r0 = stream * tt + blk * HG_BLOCK
        st_base = stream * heads
        for h in range(heads):
            cs = slice(h * hd, (h + 1) * hd)
            lc = lc_s[h, pl.ds(r0, HG_BLOCK), :]
            lck = lck_s[h, pl.ds(r0, HG_BLOCK), :]
            q = q_s[h, pl.ds(r0, HG_BLOCK), :]
            v = v_s[h, pl.ds(r0, HG_BLOCK), :]
            lc_last = lc[HG_BLOCK - 1:HG_BLOCK, :]
            q_dec = q * jnp.exp2(lc)
            k_end = jnp.exp2(lc_last - lck)
            g_blk = jnp.exp2(lc_last)
            st = st_ref[st_base + h]
            o = lax.dot_general(q_dec.astype(BF16), st.astype(BF16), (((1,), (1,)), ((), ())),
                                preferred_element_type=F32)
            q_top, q_bot = q[:half], q[half:]
            lc_top, lc_bot = lc[:half], lc[half:]
            o_top = None
            o_bot = None
            for s in range(HG_BLOCK):
                c = lck_s[h, pl.ds(r0 + s, 1), :]
                vs = v_s[h, pl.ds(r0 + s, 1), :]
                if s < half:
                    top = q_top * jnp.exp2(lc_top - c)
                    if s > 0:
                        top = jnp.where(row_id >= s, top, 0.0)
                    top = jnp.sum(top, axis=-1, keepdims=True) * vs
                    o_top = top if o_top is None else o_top + top
                bot = q_bot * jnp.exp2(lc_bot - c)
                if s > half:
                    bot = jnp.where(row_id >= s - half, bot, 0.0)
                bot = jnp.sum(bot, axis=-1, keepdims=True) * vs
                o_bot = bot if o_bot is None else o_bot + bot
            o_s[pl.ds(r0, HG_BLOCK), cs] = o + jnp.concatenate([o_top, o_bot], axis=0)
            d_st = lax.dot_general(v.astype(BF16), k_end.astype(BF16), (((0,), (0,)), ((), ())),
                                   preferred_element_type=F32)
            st_ref[st_base + h] = st * g_blk + d_st

    for stream in range(n_streams):
        for blk in range(tt // HG_BLOCK):
            block_body(stream, blk)

    g_act = _silu(tile(g_ref))
    for h in range(heads):
        cs = slice(h * hd, (h + 1) * hd)
        o = o_s[:, cs]
        ms = jnp.mean(o * o, axis=-1, keepdims=True)
        on = o * lax.rsqrt(ms + LN_EPS) * ng_ref[...]
        out_ref[:, :, cs] = (on * g_act[:, cs]).astype(out_ref.dtype).reshape(n_streams, tt, hd)

    @pl.when(t == nt - 1)
    def _():
        for n in range(n_streams):
            for h in range(heads):
                sout_ref[n, h] = st_ref[n * heads + h].T


def _hgrn_branch(proj, s0, lbp, ng, *, nb_batch, seq, tt, n_streams, col0):
    w = lbp.shape[1]
    heads = w // HG_HEAD_DIM
    nt = seq // tt
    rows_all = n_streams * tt
    chunk = min(rows_all, 128)
    proj3 = proj.reshape(nb_batch, seq, proj.shape[1])
    col = lambda k: pl.BlockSpec((n_streams, tt, w), lambda g, t: (g, t, col0 + k))
    state_spec = pl.BlockSpec((n_streams, heads, HG_HEAD_DIM, HG_HEAD_DIM), lambda g, t: (g, 0, 0, 0))
    slab = pltpu.VMEM((heads, rows_all, HG_HEAD_DIM), F32)
    out, state = pl.pallas_call(
        functools.partial(_hgrn_kernel, tt=tt, nt=nt, heads=heads, chunk=chunk, n_streams=n_streams),
        grid=(nb_batch // n_streams, nt),
        in_specs=[
            col(0), col(1), col(2), col(3),
            state_spec,
            pl.BlockSpec(lbp.shape, lambda g, t: (0, 0)),
            pl.BlockSpec((1, HG_HEAD_DIM), lambda g, t: (0, 0)),
        ],
        out_specs=[
            pl.BlockSpec((n_streams, tt, w), lambda g, t: (g, t, 0)),
            state_spec,
        ],
        out_shape=[
            jax.ShapeDtypeStruct((nb_batch, seq, w), BF16),
            jax.ShapeDtypeStruct((nb_batch, heads, HG_HEAD_DIM, HG_HEAD_DIM), F32),
        ],
        scratch_shapes=[
            pltpu.VMEM((n_streams * heads, HG_HEAD_DIM, HG_HEAD_DIM), F32),
            slab, slab, slab, slab,
            pltpu.VMEM((rows_all, w), F32),
        ],
        compiler_params=_params(("arbitrary", "arbitrary")),
        name="hgrn_branch",
    )(proj3, proj3, proj3, proj3, s0, lbp, ng)
    return out.reshape(nb_batch * seq, w), state


def _out_proj_kernel(x_ref, c_ref, h_ref, wc_ref, wh_ref, g_ref, b_ref, o_ref, *, alpha, rows_sub):
    for r0 in range(0, x_ref.shape[0], rows_sub):
        rows = slice(r0, r0 + rows_sub)
        mixed = (jnp.dot(c_ref[rows, :], wc_ref[...], preferred_element_type=F32)
                 + jnp.dot(h_ref[rows, :], wh_ref[...], preferred_element_type=F32))
        o_ref[rows, :] = _layer_norm(alpha * x_ref[rows, :] + mixed, g_ref[...], b_ref[...])


def _out_proj(x, conv, hg, w_bf, g, b, alpha, *, tm, rows_sub=OUT_PROJ_SUB_ROWS):
    m, d = x.shape
    c = conv.shape[1]
    row = lambda i: (i, 0)
    fixed = lambda i: (0, 0)
    return pl.pallas_call(
        functools.partial(_out_proj_kernel, alpha=alpha, rows_sub=rows_sub),
        grid=(m // tm,),
        in_specs=[
            pl.BlockSpec((tm, d), row),
            pl.BlockSpec((tm, c), row),
            pl.BlockSpec((tm, c), row),
            pl.BlockSpec((c, d), fixed),
            pl.BlockSpec((c, d), lambda i: (1, 0)),
            pl.BlockSpec((1, d), fixed),
            pl.BlockSpec((1, d), fixed),
        ],
        out_specs=pl.BlockSpec((tm, d), row),
        out_shape=jax.ShapeDtypeStruct((m, d), F32),
        compiler_params=_params(("arbitrary",)),
        name="out_proj",
    )(x, conv, hg, w_bf, w_bf, g, b)


def _ffn_kernel(h_ref, wg_ref, wu_ref, wd_ref, g_ref, b_ref, y_ref, *, n_ff_tiles, alpha):
    j = pl.program_id(1)

    hb = h_ref[...].astype(BF16)
    gate = jnp.dot(hb, wg_ref[...], preferred_element_type=F32)
    up = jnp.dot(hb, wu_ref[...], preferred_element_type=F32)
    act = (_silu(gate) * up).astype(BF16)
    acc = jnp.where(j == 0, 0.0, y_ref[...])
    y_ref[...] = acc + jnp.dot(act, wd_ref[...], preferred_element_type=F32)

    @pl.when(j == n_ff_tiles - 1)
    def _():
        y_ref[...] = _layer_norm(alpha * h_ref[...] + y_ref[...], g_ref[...], b_ref[...])


def _ffn(h, wg_bf, wu_bf, wd_bf, g, b, alpha, *, tm, tf):
    m, d = h.shape
    dff = wg_bf.shape[1]
    nj = dff // tf
    up_spec = pl.BlockSpec((d, tf), lambda i, j: (0, j))
    fixed = lambda i, j: (0, 0)
    return pl.pallas_call(
        functools.partial(_ffn_kernel, n_ff_tiles=nj, alpha=alpha),
        grid=(m // tm, nj),
        in_specs=[
            pl.BlockSpec((tm, d), lambda i, j: (i, 0)),
            up_spec, up_spec,
            pl.BlockSpec((tf, d), lambda i, j: (j, 0)),
            pl.BlockSpec((1, d), fixed),
            pl.BlockSpec((1, d), fixed),
        ],
        out_specs=pl.BlockSpec((tm, d), lambda i, j: (i, 0)),
        out_shape=jax.ShapeDtypeStruct((m, d), F32),
        compiler_params=_params(("arbitrary", "arbitrary"), VMEM_LIMIT_DENSE),
        name="ffn",
    )(h, wg_bf, wu_bf, wd_bf, g, b)


def kernel(x_prompt, x_sample, cache_conv, state_hgrn, w_in, b_in, w_dw, b_dw, conv_norm_g, conv_norm_b,
           hg_lower_bounds, hg_norm_g, w_out, ln1_g, ln1_b, w_gate, w_up, w_down, ln2_g, ln2_b):
    depth = w_in.shape[0]
    assert depth == 1, "single-layer step"
    bp, seq, d = x_prompt.shape
    bs, dseq, _ = x_sample.shape
    conv_w = w_dw.shape[2]
    conv_state = w_dw.shape[1] - 1
    hg_w = hg_lower_bounds.shape[1]
    heads = hg_w // HG_HEAD_DIM
    assert conv_w == hg_w and w_in.shape[2] == 2 * conv_w + 4 * hg_w
    assert bs * dseq == ROW_TILE, "the sample rows form one dense-stage row tile"
    alpha = (2.0 * depth) ** 0.25

    xp = x_prompt.reshape(bp * seq, d)
    xs = x_sample.reshape(bs * dseq, d)
    row = lambda a: a.reshape(1, -1)

    proj_s, w_in_bf = _in_proj_cast(xs, w_in[0], row(b_in[0]), tk=IN_PROJ_CAST_ROWS)
    proj_p = _in_proj(xp, w_in_bf, row(b_in[0]), tm=IN_PROJ_PROMPT_ROWS, tn=IN_PROJ_PROMPT_COLS)

    hist_p = jnp.zeros((bp, conv_state, conv_w), F32)
    conv_args = (w_dw[0], row(b_dw[0]), row(conv_norm_g[0]), row(conv_norm_b[0]))
    conv_p, tail_p, (w_out_bf, wg_bf, wu_bf, wd_bf) = _conv_branch(
        proj_p, hist_p, *conv_args, nb_batch=bp, seq=seq, tt=CONV_PROMPT_ROWS,
        cast=(w_out[0], w_gate[0], w_up[0], w_down[0]))
    conv_s, tail_s, _ = _conv_branch(proj_s, cache_conv[0], *conv_args, nb_batch=bs, seq=dseq, tt=dseq,
                                     sps=CONV_SAMPLE_STREAMS)

    s0_p = jnp.zeros((bp, heads, HG_HEAD_DIM, HG_HEAD_DIM), F32)
    hg_args = (hg_lower_bounds, row(hg_norm_g[0]))
    hg_s, st_s = _hgrn_branch(proj_s, state_hgrn[0], *hg_args, nb_batch=bs, seq=dseq, tt=dseq,
                              n_streams=HGRN_SAMPLE_STREAMS, col0=2)
    hg_p, st_p = _hgrn_branch(proj_p, s0_p, *hg_args, nb_batch=bp, seq=seq, tt=HGRN_PROMPT_ROWS,
                              n_streams=HGRN_PROMPT_STREAMS, col0=2)

    out_args = (w_out_bf, row(ln1_g[0]), row(ln1_b[0]), alpha)
    h_s = _out_proj(xs, conv_s, hg_s, *out_args, tm=ROW_TILE)
    h_p = _out_proj(xp, conv_p, hg_p, *out_args, tm=ROW_TILE)

    ffn_args = (wg_bf, wu_bf, wd_bf, row(ln2_g[0]), row(ln2_b[0]), alpha)
    ys = _ffn(h_s, *ffn_args, tm=ROW_TILE, tf=FFN_TILE)
    yp = _ffn(h_p, *ffn_args, tm=FFN_PROMPT_ROWS, tf=FFN_TILE)

    return (yp.reshape(bp, seq, d), ys.reshape(bs, dseq, d),
            tail_p[None], st_p[None].astype(x_prompt.dtype),
            tail_s[None].astype(cache_conv.dtype), st_s[None].astype(state_hgrn.dtype))
```
